```python
import math
import jax, jax.numpy as jnp
from jax import lax
import numpy as np

D_MODEL = 2048
BATCH = 2
SEQ = 16384
DEPTH = 1

CHUNK = 64
NORM_EPS = 1e-6

RWKV_HEADS = 16
RWKV_HEAD_DIM = 64
RWKV_WIDTH = RWKV_HEADS * RWKV_HEAD_DIM
DECAY_LORA = 96
AAA_LORA = 96
GATE_LORA = 256
RWKV_GN_EPS = 64e-5

DIFF_HEADS = 8
DIFF_QK_DIM = 64
DIFF_V_DIM = 2 * DIFF_QK_DIM
DIFF_QK_WIDTH = DIFF_HEADS * 2 * DIFF_QK_DIM
DIFF_V_WIDTH = DIFF_HEADS * DIFF_V_DIM
ROPE_THETA = 500000.0
ROPE_DIM = DIFF_QK_DIM // 4
Q_BLOCK = 128

N_EXPERTS = 32
TOP_K = 4
D_FF = D_MODEL
SWIGLU_LIMIT = 7.0
SWIGLU_ALPHA = 1.702
MOE_BLOCK = 128

SHIFT_SIZES = (RWKV_WIDTH, RWKV_WIDTH, RWKV_WIDTH, DECAY_LORA, AAA_LORA, GATE_LORA)
REST_SIZES = (DIFF_QK_WIDTH, DIFF_QK_WIDTH, DIFF_V_WIDTH, D_MODEL, D_MODEL)
SHIFT_WIDTH = 3 * RWKV_WIDTH + DECAY_LORA + AAA_LORA + GATE_LORA
IN_WIDTH = SHIFT_WIDTH + 2 * DIFF_QK_WIDTH + DIFF_V_WIDTH + 2 * D_MODEL

kernel_name = 'hybrid_rwkv7_diffattn_moe_block'


def _split(t, sizes):
    outs, start = [], 0
    for s in sizes:
        outs.append(t[..., start:start + s])
        start += s
    return outs


def rms_norm(x, gain, eps=NORM_EPS):
    xf = x.astype(jnp.float32)
    y = xf * lax.rsqrt(jnp.mean(xf * xf, axis=-1, keepdims=True) + eps)
    return (y * gain.astype(jnp.float32)).astype(x.dtype)


def token_shift(p, mu):
    prev = jnp.pad(p, ((0, 0), (1, 0), (0, 0)))[:, :-1]
    return p + (prev - p) * mu


def partial_rope(x, pos):
    half = ROPE_DIM // 2
    inv_freq = ROPE_THETA ** (-jnp.arange(half, dtype=jnp.float32) * 2.0 / ROPE_DIM)
    ang = pos.astype(jnp.float32)[:, None] * inv_freq[None, :]
    cos = jnp.cos(ang)[None, :, None, None, :]
    sin = jnp.sin(ang)[None, :, None, None, :]
    xr = x[..., :ROPE_DIM].astype(jnp.float32)
    x1, x2 = xr[..., :half], xr[..., half:]
    rot = jnp.concatenate([x1 * cos - x2 * sin, x2 * cos + x1 * sin], axis=-1)
    return jnp.concatenate([rot.astype(x.dtype), x[..., ROPE_DIM:]], axis=-1)


def rwkv7_branch(r, k, v, w_d, a_d, g_d, w0, w_decay_up, a0, w_aaa_up, w_gate_up,
                 k_k, k_a, r_k, gn_w, gn_b):
    B, S, _ = r.shape
    H, N = RWKV_HEADS, RWKV_HEAD_DIM
    f32 = jnp.float32
    w_log = -jax.nn.softplus(-(w0 + jnp.tanh(w_d) @ w_decay_up)) - 0.5
    a = jax.nn.sigmoid(a0 + a_d @ w_aaa_up)
    g = jax.nn.sigmoid(g_d) @ w_gate_up
    kk = (k * k_k).reshape(B, S, H, N).astype(f32)
    kk = kk / jnp.maximum(jnp.sqrt(jnp.sum(kk * kk, axis=-1, keepdims=True)), 1e-12)
    k = k * (1.0 + (a - 1.0) * k_a)

    def heads(t):
        return t.reshape(B, S, H, N).astype(f32)

    rh, kh, vh, ah = heads(r), heads(k), heads(v), heads(a)
    decay = jnp.exp(-jnp.exp(heads(w_log)))
    a_vec = -kk
    b_vec = kk * ah

    def step(state, inp):
        r_t, d_t, k_t, v_t, a_t, b_t = inp
        sa = jnp.einsum('bhvk,bhk->bhv', state, a_t)
        state = (state * d_t[:, :, None, :] + sa[..., None] * b_t[:, :, None, :]
                 + v_t[..., None] * k_t[:, :, None, :])
        y = jnp.einsum('bhvk,bhk->bhv', state, r_t)
        return state, y

    xs = tuple(jnp.moveaxis(t, 1, 0) for t in (rh, decay, kh, vh, a_vec, b_vec))
    _, ys = lax.scan(step, jnp.zeros((B, H, N, N), f32), xs)
    y = jnp.moveaxis(ys, 0, 1)
    mu = jnp.mean(y, axis=-1, keepdims=True)
    var = jnp.mean(jnp.square(y - mu), axis=-1, keepdims=True)
    yn = (y - mu) * lax.rsqrt(var + RWKV_GN_EPS)
    yn = yn * gn_w.reshape(H, N).astype(f32) + gn_b.reshape(H, N).astype(f32)
    bonus = jnp.sum(rh * kh * r_k.reshape(H, N).astype(f32), axis=-1, keepdims=True) * vh
    out = (yn + bonus).reshape(B, S, H * N).astype(r.dtype) * g
    return out


def diff_attention(q, k, v, q_gain, k_gain, lam_q1, lam_k1, lam_q2, lam_k2, subln_gain,
                   lambda_init):
    B, S, _ = q.shape
    H, Dh, Dv = DIFF_HEADS, DIFF_QK_DIM, DIFF_V_DIM
    f32 = jnp.float32
    q = rms_norm(q.reshape(B, S, H, 2, Dh), q_gain)
    k = rms_norm(k.reshape(B, S, H, 2, Dh), k_gain)
    v = v.reshape(B, S, H, Dv)
    pos = jnp.arange(S, dtype=jnp.int32)
    q = partial_rope(q, pos)
    k = partial_rope(k, pos)
    lam = (jnp.exp(jnp.sum(lam_q1.astype(f32) * lam_k1.astype(f32)))
           - jnp.exp(jnp.sum(lam_q2.astype(f32) * lam_k2.astype(f32))) + lambda_init)
    scale = Dh ** -0.5
    n_blocks = S // Q_BLOCK
    qb = q.reshape(B, n_blocks, Q_BLOCK, H, 2, Dh).transpose(1, 0, 2, 3, 4, 5)
    key_chunk = pos // CHUNK

    def attend(args):
        q_blk, blk = args
        q_chunk = (blk * Q_BLOCK + jnp.arange(Q_BLOCK, dtype=jnp.int32)) // CHUNK
        s = jnp.einsum('bqhmd,bkhmd->bhmqk', q_blk, k, preferred_element_type=f32) * scale
        mask = key_chunk[None, :] <= q_chunk[:, None]
        s = jnp.where(mask, s, -jnp.inf)
        p = jax.nn.softmax(s, axis=-1)
        attn = p[:, :, 0] - lam * p[:, :, 1]
        return jnp.einsum('bhqk,bkhd->bqhd', attn.astype(v.dtype), v)

    o = lax.map(attend, (qb, jnp.arange(n_blocks, dtype=jnp.int32)))
    o = o.transpose(1, 0, 2, 3, 4).reshape(B, S, H, Dv)
    o = rms_norm(o, subln_gain) * (1.0 - lambda_init)
    return o.reshape(B, S, H * Dv)


def moe_ffn(h, w_router, b_router, w_gate, b_gate, w_up, b_up, w_down, b_down):
    B, S, D = h.shape
    T = B * S
    M = T * TOP_K
    n_blocks = (M + N_EXPERTS * (MOE_BLOCK - 1) + MOE_BLOCK - 1) // MOE_BLOCK
    P = n_blocks * MOE_BLOCK
    hf = h.reshape(T, D)
    logits = (hf @ w_router + b_router).astype(jnp.float32)
    top_val, top_idx = lax.top_k(logits, TOP_K)
    top_w = jax.nn.softmax(top_val, axis=-1)

    flat_e = top_idx.reshape(M).astype(jnp.int32)
    order = jnp.argsort(flat_e)
    sorted_e = flat_e[order]
    counts = jnp.bincount(flat_e, length=N_EXPERTS)
    padded = (counts + MOE_BLOCK - 1) // MOE_BLOCK * MOE_BLOCK
    pad_end = jnp.cumsum(padded)
    pad_start = pad_end - padded
    sort_start = jnp.cumsum(counts) - counts
    dest = pad_start[sorted_e] + jnp.arange(M, dtype=jnp.int32) - sort_start[sorted_e]
    row_token = jnp.full((P,), T, jnp.int32).at[dest].set((order // TOP_K).astype(jnp.int32))
    row_gate = jnp.zeros((P,), jnp.float32).at[dest].set(top_w.reshape(M)[order])
    block_start = jnp.arange(n_blocks, dtype=jnp.int32) * MOE_BLOCK
    block_expert = jnp.minimum(jnp.searchsorted(pad_end, block_start, side='right'),
                               N_EXPERTS - 1).astype(jnp.int32)
    h_pad = jnp.concatenate([hf, jnp.zeros((1, D), hf.dtype)], axis=0)

    def expert_block(acc, args):
        tok, gw, e = args
        xb = h_pad[tok]
        gt = jnp.minimum(xb @ w_gate[e] + b_gate[e], SWIGLU_LIMIT)
        up = jnp.clip(xb @ w_up[e] + b_up[e], -SWIGLU_LIMIT, SWIGLU_LIMIT)
        y = ((up + 1.0) * gt * jax.nn.sigmoid(SWIGLU_ALPHA * gt)) @ w_down[e] + b_down[e]
        acc = acc.at[tok].add(y * gw[:, None].astype(y.dtype))
        return acc, None

    acc, _ = lax.scan(expert_block, jnp.zeros((T + 1, D), h.dtype),
                      (row_token.reshape(n_blocks, MOE_BLOCK),
                       row_gate.reshape(n_blocks, MOE_BLOCK), block_expert))
    return acc[:T].reshape(B, S, D)


def setup_inputs(seed: int = 0) -> dict:
    key = jax.random.key(seed)
    ks = iter(jax.random.split(key, 48))
    L, D, E, F = DEPTH, D_MODEL, N_EXPERTS, D_FF

    def nrm(shape, scale):
        return scale * jax.random.normal(next(ks), shape, jnp.float32)

    def gain(shape):
        return 1.0 + nrm(shape, 0.02)

    def unif(shape, lo, hi):
        return jax.random.uniform(next(ks), shape, jnp.float32, lo, hi)

    return {
        'x': nrm((BATCH, SEQ, D), 1.0),
        'c': nrm((BATCH, D), 1.0),
        'w_ada': nrm((L, D, 6 * D), 0.5 * D ** -0.5),
        'b_ada': nrm((L, 6 * D), 0.02),
        'norm1_gain': gain((L, D)),
        'norm2_gain': gain((L, D)),
        'w_in': nrm((L, D, IN_WIDTH), D ** -0.5),
        'shift_mu': unif((L, SHIFT_WIDTH), 0.0, 1.0),
        'w0': unif((L, RWKV_WIDTH), -5.0, 0.0),
        'w_decay_up': nrm((L, DECAY_LORA, RWKV_WIDTH), 0.1 * DECAY_LORA ** -0.5),
        'a0': nrm((L, RWKV_WIDTH), 0.1),
        'w_aaa_up': nrm((L, AAA_LORA, RWKV_WIDTH), 0.5 * AAA_LORA ** -0.5),
        'w_gate_up': nrm((L, GATE_LORA, RWKV_WIDTH), GATE_LORA ** -0.5),
        'k_k': 0.85 + nrm((L, RWKV_WIDTH), 0.02),
        'k_a': gain((L, RWKV_WIDTH)),
        'r_k': nrm((L, RWKV_WIDTH), 0.1),
        'gn_w': gain((L, RWKV_WIDTH)),
        'gn_b': nrm((L, RWKV_WIDTH), 0.02),
        'q_gain': gain((L, DIFF_QK_DIM)),
        'k_gain': gain((L, DIFF_QK_DIM)),
        'lam_q1': nrm((L, DIFF_QK_DIM), 0.1),
        'lam_k1': nrm((L, DIFF_QK_DIM), 0.1),
        'lam_q2': nrm((L, DIFF_QK_DIM), 0.1),
        'lam_k2': nrm((L, DIFF_QK_DIM), 0.1),
        'subln_gain': gain((L, DIFF_V_DIM)),
        'w_branch_a': nrm((L, RWKV_WIDTH, D), RWKV_WIDTH ** -0.5),
        'w_branch_b': nrm((L, DIFF_V_WIDTH, D), DIFF_V_WIDTH ** -0.5),
        'w_out': nrm((L, D, D), D ** -0.5),
        'w_router': nrm((L, D, E), D ** -0.5),
        'b_router': nrm((L, E), 0.01),
        'w_exp_gate': nrm((L, E, D, F), D ** -0.5),
        'b_exp_gate': nrm((L, E, F), 0.01),
        'w_exp_up': nrm((L, E, D, F), D ** -0.5),
        'b_exp_up': nrm((L, E, F), 0.01),
        'w_exp_down': nrm((L, E, F, D), F ** -0.5),
        'b_exp_down': nrm((L, E, D), 0.01),
    }


def reference(x, c, w_ada, b_ada, norm1_gain, norm2_gain, w_in, shift_mu, w0, w_decay_up,
              a0, w_aaa_up, w_gate_up, k_k, k_a, r_k, gn_w, gn_b, q_gain, k_gain,
              lam_q1, lam_k1, lam_q2, lam_k2, subln_gain, w_branch_a, w_branch_b, w_out,
              w_router, b_router, w_exp_gate, b_exp_gate, w_exp_up, b_exp_up,
              w_exp_down, b_exp_down):
    for l in range(DEPTH):
        mod = jax.nn.silu(c) @ w_ada[l] + b_ada[l]
        sh1, sc1, g1, sh2, sc2, g2 = jnp.split(mod[:, None, :], 6, axis=-1)

        h = rms_norm(x, norm1_gain[l]) * (1.0 + sc1) + sh1
        proj = h @ w_in[l]
        shifted = token_shift(proj[..., :SHIFT_WIDTH], shift_mu[l])
        r, k, v, w_d, a_d, g_d = _split(shifted, SHIFT_SIZES)
        qd, kd, vd, gate_a, gate_b = _split(proj[..., SHIFT_WIDTH:], REST_SIZES)

        o_a = rwkv7_branch(r, k, v, w_d, a_d, g_d, w0[l], w_decay_up[l], a0[l], w_aaa_up[l],
                           w_gate_up[l], k_k[l], k_a[l], r_k[l], gn_w[l], gn_b[l])
        lambda_init = 0.8 - 0.6 * math.exp(-0.3 * l)
        o_b = diff_attention(qd, kd, vd, q_gain[l], k_gain[l], lam_q1[l], lam_k1[l],
                             lam_q2[l], lam_k2[l], subln_gain[l], lambda_init)

        merged = (jax.nn.sigmoid(gate_a) * (o_a @ w_branch_a[l])
                  + jax.nn.sigmoid(gate_b) * (o_b @ w_branch_b[l]))
        x = x + g1 * (merged @ w_out[l])

        h2 = rms_norm(x, norm2_gain[l]) * (1.0 + sc2) + sh2
        x = x + g2 * moe_ffn(h2, w_router[l], b_router[l], w_exp_gate[l], b_exp_gate[l],
                             w_exp_up[l], b_exp_up[l], w_exp_down[l], b_exp_down[l])
    return x
```

```python
import functools
import math

import jax
import jax.numpy as jnp
from jax import lax
from jax.experimental import pallas as pl
from jax.experimental.pallas import tpu as pltpu

F32 = jnp.float32
BF16 = jnp.bfloat16

NORM_EPS = 1e-6
CHUNK = 64

RWKV_HEADS = 16
RWKV_HEAD_DIM = 64
RWKV_WIDTH = RWKV_HEADS * RWKV_HEAD_DIM
RWKV_GN_EPS = 64e-5

DIFF_HEADS = 8
DIFF_QK_DIM = 64
DIFF_V_DIM = 128
DIFF_WIDTH = DIFF_HEADS * DIFF_V_DIM
ROPE_THETA = 500000.0
ROPE_DIM = DIFF_QK_DIM // 4

TOP_K = 4
SWIGLU_LIMIT = 7.0
SWIGLU_ALPHA = 1.702

LANES = 128
LORA_PACK = 512
VMEM_LIMIT = 56 * 1024 * 1024

OFF_R, OFF_K, OFF_V = 0, 1024, 2048
OFF_QD, OFF_KD, OFF_VD = 3072, 4096, 5120
OFF_GA, OFF_GB = 6144, 8192
OFF_LORA = 10240
PACKED_WIDTH = OFF_LORA + LORA_PACK


def _cparams(sem):
    return pltpu.CompilerParams(dimension_semantics=sem, vmem_limit_bytes=VMEM_LIMIT)


def _nt_dot(a, b):
    return lax.dot_general(a, b, (((1,), (1,)), ((), ())), preferred_element_type=F32)


def _dot(a, b):
    return jnp.dot(a, b, preferred_element_type=F32)


def _ada_kernel(c_ref, w_ref, b_ref, o_ref):
    c = c_ref[...]
    s = c * jax.nn.sigmoid(c)
    o_ref[...] = _dot(s.astype(BF16), w_ref[...].astype(BF16)) + b_ref[...]


def _ada(c, w, b):
    bsz, d = c.shape
    n = w.shape[1]
    rows = 8
    tn = 1024
    cp = jnp.pad(c, ((0, rows - bsz), (0, 0)))
    out = pl.pallas_call(
        _ada_kernel,
        grid=(n // tn,),
        in_specs=[pl.BlockSpec((rows, d), lambda j: (0, 0)),
                  pl.BlockSpec((d, tn), lambda j: (0, j)),
                  pl.BlockSpec((1, tn), lambda j: (0, j))],
        out_specs=pl.BlockSpec((rows, tn), lambda j: (0, j)),
        out_shape=jax.ShapeDtypeStruct((rows, n), F32),
        compiler_params=_cparams(("arbitrary",)),
        name="ada",
    )(cp, w, b.reshape(1, n))
    return out[:bsz]


def _inproj_kernel(x_ref, g_ref, sc_ref, sh_ref, w_ref, mu_ref, o_ref, h_scr, carry_scr,
                   *, tiles_per_batch, n_rkv_tiles, lora_tile):
    i = pl.program_id(0)
    j = pl.program_id(1)

    @pl.when(j == 0)
    def _():
        x = x_ref[...]
        ms = jnp.mean(x * x, axis=-1, keepdims=True)
        y = x * lax.rsqrt(ms + NORM_EPS) * g_ref[...]
        h_scr[...] = (y * (1.0 + sc_ref[0]) + sh_ref[0]).astype(BF16)

    p = _dot(h_scr[...], w_ref[...])
    shifted = jnp.logical_or(j < n_rkv_tiles, j == lora_tile)

    @pl.when(shifted)
    def _():
        slot = jnp.where(j == lora_tile, n_rkv_tiles, j)

        @pl.when(i % tiles_per_batch == 0)
        def _():
            carry_scr[slot] = jnp.zeros(carry_scr.shape[1:], F32)

        prev_last = carry_scr[slot]
        tm = p.shape[0]
        carry_scr[slot] = p[tm - 1:tm, :]
        row = lax.broadcasted_iota(jnp.int32, p.shape, 0)
        prev = jnp.where(row == 0, prev_last, pltpu.roll(p, 1, 0))
        o_ref[...] = (p + (prev - p) * mu_ref[...]).astype(o_ref.dtype)

    @pl.when(jnp.logical_not(shifted))
    def _():
        o_ref[...] = p.astype(o_ref.dtype)


def _inproj(x2, gain, sc, sh, w_packed, mu_packed, seq):
    t, d = x2.shape
    tm = min(1024, seq)
    tn = 512
    n_col = PACKED_WIDTH // tn
    n_rkv_tiles = OFF_QD // tn
    lora_tile = OFF_LORA // tn
    tiles_per_batch = seq // tm

    def mu_map(i, j):
        return (0, jnp.where(j == lora_tile, n_rkv_tiles, jnp.minimum(j, n_rkv_tiles - 1)))

    kern = functools.partial(_inproj_kernel, tiles_per_batch=tiles_per_batch,
                             n_rkv_tiles=n_rkv_tiles, lora_tile=lora_tile)
    return pl.pallas_call(
        kern,
        grid=(t // tm, n_col),
        in_specs=[pl.BlockSpec((tm, d), lambda i, j: (i, 0)),
                  pl.BlockSpec((1, d), lambda i, j: (0, 0)),
                  pl.BlockSpec((1, 1, d), lambda i, j: (i // tiles_per_batch, 0, 0)),
                  pl.BlockSpec((1, 1, d), lambda i, j: (i // tiles_per_batch, 0, 0)),
                  pl.BlockSpec((d, tn), lambda i, j: (0, j)),
                  pl.BlockSpec((1, tn), mu_map)],
        out_specs=pl.BlockSpec((tm, tn), lambda i, j: (i, j)),
        out_shape=jax.ShapeDtypeStruct((t, PACKED_WIDTH), BF16),
        scratch_shapes=[pltpu.VMEM((tm, d), BF16),
                        pltpu.VMEM((n_rkv_tiles + 1, 1, tn), F32)],
        compiler_params=_cparams(("arbitrary", "arbitrary")),
        name="inproj",
    )(x2, gain, sc, sh, w_packed, mu_packed)


def _pack_in_weights(w_in, shift_mu, decay_rank, aaa_rank, gate_rank):
    assert decay_rank <= LANES and aaa_rank <= LANES and gate_rank == 2 * LANES
    rw = RWKV_WIDTH
    o = 3 * rw
    w_r, w_k, w_v = w_in[:, 0:rw], w_in[:, rw:2 * rw], w_in[:, 2 * rw:3 * rw]
    w_dl = w_in[:, o:o + decay_rank]
    w_al = w_in[:, o + decay_rank:o + decay_rank + aaa_rank]
    w_gl = w_in[:, o + decay_rank + aaa_rank:o + decay_rank + aaa_rank + gate_rank]
    rest = w_in[:, o + decay_rank + aaa_rank + gate_rank:]

    def padc(a, width):
        return jnp.pad(a, ((0, 0), (0, width - a.shape[1])))

    packed = jnp.concatenate(
        [w_r, w_k, w_v, rest, padc(w_dl, LANES), padc(w_al, LANES), w_gl], axis=1).astype(BF16)
    assert packed.shape[1] == PACKED_WIDTH
    mu = shift_mu.reshape(1, -1)
    mu_packed = jnp.concatenate(
        [mu[:, :o], padc(mu[:, o:o + decay_rank], LANES),
         padc(mu[:, o + decay_rank:o + decay_rank + aaa_rank], LANES),
         mu[:, o + decay_rank + aaa_rank:]], axis=1)
    return packed, mu_packed


def _split3_bf16(x):
    hi = x.astype(BF16)
    r1 = x - hi.astype(F32)
    mid = r1.astype(BF16)
    lo = (r1 - mid.astype(F32)).astype(BF16)
    return hi, mid, lo


def _rwkv_kernel(r_ref, k_ref, v_ref, lora_ref, wd_ref, wa_ref, wg_ref, prm_ref, o_ref, st_ref,
                 *, tm):
    c = CHUNK
    n = RWKV_HEAD_DIM
    n_chunks = tm // c

    @pl.when(pl.program_id(2) == 0)
    def _():
        st_ref[...] = jnp.zeros(st_ref.shape, F32)

    lane = lax.broadcasted_iota(jnp.int32, (1, LANES), 1)
    m0 = lane < n
    m0_256 = jnp.concatenate([m0, m0], axis=1)

    def head_sum(x):
        s0 = jnp.sum(jnp.where(m0, x, 0.0), axis=-1, keepdims=True)
        s1 = jnp.sum(jnp.where(m0, 0.0, x), axis=-1, keepdims=True)
        return jnp.where(m0, s0, s1)

    prm = prm_ref[...]
    w0, a0, k_k, k_a, r_k, gn_w, gn_b = (prm[i:i + 1] for i in range(7))

    r = r_ref[0].astype(F32)
    k = k_ref[0].astype(F32)
    v = v_ref[0].astype(F32)
    lora = lora_ref[0]
    d_code = lora[:, 0:LANES].astype(F32)
    a_code = lora[:, LANES:2 * LANES]
    g_code = lora[:, 2 * LANES:4 * LANES].astype(F32)

    wl = w0 + _dot(jnp.tanh(d_code).astype(BF16), wd_ref[...])
    z = -wl
    softplus = jnp.maximum(z, 0.0) + jnp.log(1.0 + jnp.exp(-jnp.abs(z)))
    lw = -jnp.exp(-softplus - 0.5)
    a = jax.nn.sigmoid(a0 + _dot(a_code, wa_ref[...]))
    g = _dot(jax.nn.sigmoid(g_code).astype(BF16), wg_ref[...])

    kk = k * k_k
    kk = kk / jnp.maximum(jnp.sqrt(head_sum(kk * kk)), 1e-12)
    k2 = k * (1.0 + (a - 1.0) * k_a)
    a_vec = -kk
    b_vec = kk * a
    bonus = head_sum(r * k2 * r_k) * v

    ri = lax.broadcasted_iota(jnp.int32, (tm, tm), 0)
    ci = lax.broadcasted_iota(jnp.int32, (tm, tm), 1)
    tri = jnp.where(jnp.logical_and(ci <= ri, ci // c == ri // c), 1.0, 0.0).astype(BF16)
    cum3 = _dot(tri, jnp.concatenate(_split3_bf16(lw), axis=1))
    cum = cum3[:, 0:LANES] + cum3[:, LANES:2 * LANES] + cum3[:, 2 * LANES:3 * LANES]
    cum_end = jnp.concatenate(
        [jnp.broadcast_to(cum[(q + 1) * c - 1:(q + 1) * c, :], (c, LANES)) for q in range(n_chunks)],
        axis=0)

    e_inc = jnp.exp(cum)
    e_exc = jnp.exp(cum - lw)
    e_neg = jnp.exp(-cum)
    e_end = jnp.exp(cum_end - cum)
    a_t = a_vec * e_exc
    r_t = r * e_inc
    k_h = k2 * e_neg
    b_h = b_vec * e_neg
    k_d = k2 * e_end
    b_d = b_vec * e_end
    d_end = jnp.exp(cum_end)

    row64 = lax.broadcasted_iota(jnp.int32, (c, LANES), 0)
    col64 = lax.broadcasted_iota(jnp.int32, (c, LANES), 1) % n
    strict = row64 > col64
    incl = row64 >= col64
    rb = lax.broadcasted_iota(jnp.int32, (LANES, LANES), 0) < n
    cb = lax.broadcasted_iota(jnp.int32, (LANES, LANES), 1) < n
    blockdiag = rb == cb

    def stack_heads(zz, mask):
        return jnp.concatenate([jnp.where(mask, zz, 0.0), jnp.where(mask, 0.0, zz)],
                               axis=0).astype(BF16)

    state = st_ref[...]
    ys = []
    for q in range(n_chunks):
        sl = slice(q * c, (q + 1) * c)
        at_c, rt_c, kh_c, bh_c, kd_c, bd_c, v_c = (
            t[sl] for t in (a_t, r_t, k_h, b_h, k_d, b_d, v))
        lhs0 = jnp.concatenate([jnp.where(m0, at_c, 0.0), jnp.where(m0, rt_c, 0.0)], axis=0)
        lhs1 = jnp.concatenate([jnp.where(m0, 0.0, at_c), jnp.where(m0, 0.0, rt_c)], axis=0)
        rhs0 = jnp.concatenate([bh_c, kh_c], axis=0).astype(BF16)
        rhs1 = jnp.concatenate([kh_c, bh_c], axis=0).astype(BF16)
        out0 = _nt_dot(lhs0.astype(BF16), rhs0)
        out1 = _nt_dot(lhs1.astype(BF16), rhs1)
        x_cat = jnp.where(strict, jnp.where(m0, out0[:c], out1[:c]), 0.0)
        ak_x = jnp.where(strict, jnp.where(m0, out1[:c], out0[:c]), 0.0)
        rb_cat = jnp.where(incl, jnp.where(m0, out0[c:], out1[c:]), 0.0)
        rk_x = jnp.where(incl, jnp.where(m0, out1[c:], out0[c:]), 0.0)

        v_stack_x = jnp.concatenate([jnp.where(m0, 0.0, v_c), jnp.where(m0, v_c, 0.0)],
                                    axis=0).astype(BF16)
        w12 = _dot(jnp.concatenate([ak_x, rk_x], axis=0).astype(BF16), v_stack_x)
        w1, w2 = w12[:c], w12[c:]

        zz = jnp.concatenate([at_c, w1], axis=1)
        xp = x_cat
        for step in range(6):
            zz = zz + _dot(xp.astype(BF16), stack_heads(zz, m0_256))
            if step < 5:
                xp = _dot(xp.astype(BF16), stack_heads(xp, m0))
        rbz = _dot(rb_cat.astype(BF16), stack_heads(zz, m0_256))
        q_c = rt_c + rbz[:, 0:LANES]
        z_c = w2 + rbz[:, LANES:]

        p_c, g_c = zz[:, 0:LANES], zz[:, LANES:]
        t_src = jnp.concatenate(
            [jnp.concatenate([p_c, g_c], axis=1),
             jnp.concatenate([jnp.zeros_like(v_c), v_c], axis=1)], axis=0)
        mn = _dot(t_src.T.astype(BF16), jnp.concatenate([bd_c, kd_c], axis=0).astype(BF16))
        m_x = jnp.where(blockdiag, mn[0:LANES], 0.0)
        n_x = jnp.where(blockdiag, mn[LANES:], 0.0)

        s_bf = state.astype(BF16)
        ys.append(_nt_dot(q_c.astype(BF16), s_bf) + z_c)
        state = state * d_end[q * c:q * c + 1, :] + _dot(s_bf, m_x.astype(BF16)) + n_x

    st_ref[...] = state
    y = jnp.concatenate(ys, axis=0)
    mu = head_sum(y) * (1.0 / n)
    yc = y - mu
    var = head_sum(yc * yc) * (1.0 / n)
    yn = yc * lax.rsqrt(var + RWKV_GN_EPS) * gn_w + gn_b
    o_ref[0] = ((yn + bonus) * g).astype(o_ref.dtype)


def _rwkv(proj3, wd, wa, wg, prm, seq):
    bsz = proj3.shape[0]
    tm = min(256, seq)
    n_pairs = RWKV_WIDTH // LANES
    blk = lambda off: (lambda b, p, t: (b, t, off // LANES + p))
    kern = functools.partial(_rwkv_kernel, tm=tm)
    return pl.pallas_call(
        kern,
        grid=(bsz, n_pairs, seq // tm),
        in_specs=[pl.BlockSpec((1, tm, LANES), blk(OFF_R)),
                  pl.BlockSpec((1, tm, LANES), blk(OFF_K)),
                  pl.BlockSpec((1, tm, LANES), blk(OFF_V)),
                  pl.BlockSpec((1, tm, LORA_PACK), lambda b, p, t: (b, t, OFF_LORA // LORA_PACK)),
                  pl.BlockSpec((LANES, LANES), lambda b, p, t: (0, p)),
                  pl.BlockSpec((LANES, LANES), lambda b, p, t: (0, p)),
                  pl.BlockSpec((2 * LANES, LANES), lambda b, p, t: (0, p)),
                  pl.BlockSpec((8, LANES), lambda b, p, t: (0, p))],
        out_specs=pl.BlockSpec((1, tm, LANES), lambda b, p, t: (b, t, p)),
        out_shape=jax.ShapeDtypeStruct((bsz, seq, RWKV_WIDTH), BF16),
        scratch_shapes=[pltpu.VMEM((LANES, LANES), F32)],
        compiler_params=_cparams(("arbitrary", "arbitrary", "arbitrary")),
        name="rwkv",
    )(proj3, proj3, proj3, proj3, wd, wa, wg, prm)


def _attn_prep_kernel(q_ref, k_ref, v_ref, cos_ref, sa_ref, sb_ref, gavg_ref, qg_ref, kg_ref,
                      q2_ref, ko_ref, vo_ref):
    width = q_ref.shape[-1]
    reps = width // LANES
    tile = lambda t: jnp.concatenate([t] * reps, axis=1)
    cos_t, sin_a, sin_b = tile(cos_ref[...]), tile(sa_ref[...]), tile(sb_ref[...])
    gavg = gavg_ref[...]
    half = ROPE_DIM // 2

    def norm_rope(x, gain):
        x = x.astype(F32)
        sq = x * x
        hi = sq.astype(BF16)
        lo = (sq - hi.astype(F32)).astype(BF16)
        ms = _dot(hi, gavg) + _dot(lo, gavg)
        y = x * lax.rsqrt(ms + NORM_EPS) * gain
        return (y * cos_t + pltpu.roll(y, width - half, 1) * sin_a + pltpu.roll(y, half, 1) * sin_b)

    qn = norm_rope(q_ref[0], qg_ref[...])
    kn = norm_rope(k_ref[0], kg_ref[...])
    lane = lax.broadcasted_iota(jnp.int32, (1, LANES), 1)
    m0 = lane < DIFF_QK_DIM
    for h in range(DIFF_HEADS):
        sl = slice(h * LANES, (h + 1) * LANES)
        qh = qn[:, sl]
        q2_ref[0, h, 0] = jnp.where(m0, qh, 0.0).astype(BF16)
        q2_ref[0, h, 1] = jnp.where(m0, 0.0, qh).astype(BF16)
        ko_ref[0, h] = kn[:, sl].astype(BF16)
        vo_ref[0, h] = v_ref[0, :, sl]


def _rope_tables(seq):
    half = ROPE_DIM // 2
    inv_freq = ROPE_THETA ** (-jnp.arange(half, dtype=F32) * 2.0 / ROPE_DIM)
    ang = jnp.arange(seq, dtype=jnp.int32).astype(F32)[:, None] * inv_freq[None, :]
    cos, sin = jnp.cos(ang), jnp.sin(ang)
    ones = jnp.ones((seq, DIFF_QK_DIM - ROPE_DIM), F32)
    zeros = jnp.zeros((seq, DIFF_QK_DIM - ROPE_DIM), F32)
    zh = jnp.zeros((seq, half), F32)
    cos64 = jnp.concatenate([cos, cos, ones], axis=1)
    sa64 = jnp.concatenate([-sin, zh, zeros], axis=1)
    sb64 = jnp.concatenate([zh, sin, zeros], axis=1)
    dup = lambda t: jnp.concatenate([t, t], axis=1)
    return dup(cos64), dup(sa64), dup(sb64)


def _attn_prep(proj3, q_gain, k_gain, seq):
    bsz = proj3.shape[0]
    tm = min(512, seq)
    width = DIFF_HEADS * 2 * DIFF_QK_DIM
    cos_t, sin_a, sin_b = _rope_tables(seq)
    gi = jnp.arange(width) // DIFF_QK_DIM
    gavg = jnp.where(gi[:, None] == gi[None, :], 1.0 / DIFF_QK_DIM, 0.0).astype(BF16)
    reps = width // DIFF_QK_DIM
    qg = (jnp.tile(q_gain, reps) * (DIFF_QK_DIM ** -0.5)).reshape(1, width)
    kg = jnp.tile(k_gain, reps).reshape(1, width)
    hd = (bsz, DIFF_HEADS, seq, LANES)
    tab = pl.BlockSpec((tm, LANES), lambda b, t: (t, 0))
    return pl.pallas_call(
        _attn_prep_kernel,
        grid=(bsz, seq // tm),
        in_specs=[pl.BlockSpec((1, tm, width), lambda b, t: (b, t, OFF_QD // width)),
                  pl.BlockSpec((1, tm, width), lambda b, t: (b, t, OFF_KD // width)),
                  pl.BlockSpec((1, tm, width), lambda b, t: (b, t, OFF_VD // width)),
                  tab, tab, tab,
                  pl.BlockSpec((width, width), lambda b, t: (0, 0)),
                  pl.BlockSpec((1, width), lambda b, t: (0, 0)),
                  pl.BlockSpec((1, width), lambda b, t: (0, 0))],
        out_specs=[pl.BlockSpec((1, DIFF_HEADS, 2, tm, LANES), lambda b, t: (b, 0, 0, t, 0)),
                   pl.BlockSpec((1, DIFF_HEADS, tm, LANES), lambda b, t: (b, 0, t, 0)),
                   pl.BlockSpec((1, DIFF_HEADS, tm, LANES), lambda b, t: (b, 0, t, 0))],
        out_shape=[jax.ShapeDtypeStruct((bsz, DIFF_HEADS, 2, seq, LANES), BF16),
                   jax.ShapeDtypeStruct(hd, BF16), jax.ShapeDtypeStruct(hd, BF16)],
        compiler_params=_cparams(("arbitrary", "arbitrary")),
        name="attn_prep",
    )(proj3, proj3, proj3, cos_t, sin_a, sin_b, gavg, qg, kg)


def _flash_kernel(qi_ref, kj_ref, q_ref, k_ref, v_ref, lam_ref, sg_ref, o_ref, m_ref, acc_ref,
                  *, tq, tk, lambda_init):
    pidx = pl.program_id(2)
    i = qi_ref[pidx]
    j = kj_ref[pidx]

    @pl.when(j == 0)
    def _():
        m_ref[...] = jnp.full(m_ref.shape, -jnp.inf, F32)
        acc_ref[...] = jnp.zeros(acc_ref.shape, F32)

    q = q_ref[0, 0].reshape(2 * tq, LANES)
    s = _nt_dot(q, k_ref[0, 0])

    def update(sc):
        m_old = m_ref[...]
        m_new = jnp.maximum(m_old, jnp.max(sc, axis=-1, keepdims=True))
        alpha = jnp.exp(m_old - m_new)
        p = jnp.exp(sc - m_new).astype(BF16)
        v = v_ref[0, 0]
        v_ext = jnp.concatenate([v, jnp.ones_like(v)], axis=1)
        acc_ref[...] = alpha * acc_ref[...] + _dot(p, v_ext)
        m_ref[...] = m_new

    @pl.when(j < i)
    def _():
        update(s)

    @pl.when(j == i)
    def _():
        qpos = lax.broadcasted_iota(jnp.int32, s.shape, 0) % tq
        kpos = lax.broadcasted_iota(jnp.int32, s.shape, 1)
        update(jnp.where(kpos // CHUNK <= qpos // CHUNK, s, -jnp.inf))
        lam_rows = lam_ref[...]
        lam = (jnp.exp(jnp.sum(lam_rows[0:1] * lam_rows[1:2], axis=-1, keepdims=True))
               - jnp.exp(jnp.sum(lam_rows[2:3] * lam_rows[3:4], axis=-1, keepdims=True))
               + lambda_init)
        acc = acc_ref[...]
        o1 = acc[0:tq, 0:LANES] / acc[0:tq, LANES:]
        o2 = acc[tq:, 0:LANES] / acc[tq:, LANES:]
        o = o1 - lam * o2
        ms = jnp.mean(o * o, axis=-1, keepdims=True)
        o_ref[0] = (o * lax.rsqrt(ms + NORM_EPS) * sg_ref[...] * (1.0 - lambda_init)).astype(o_ref.dtype)


def _flash(q2, kh, vh, lam_rows, subln_gain, seq, lambda_init):
    bsz = q2.shape[0]
    tq = tk = min(1024, seq)
    nq = seq // tq
    pairs = [(i, j) for i in range(nq) for j in range(i + 1)]
    qi = jnp.asarray([p[0] for p in pairs], jnp.int32)
    kj = jnp.asarray([p[1] for p in pairs], jnp.int32)
    kern = functools.partial(_flash_kernel, tq=tq, tk=tk, lambda_init=lambda_init)
    grid_spec = pltpu.PrefetchScalarGridSpec(
        num_scalar_prefetch=2,
        grid=(bsz, DIFF_HEADS, len(pairs)),
        in_specs=[pl.BlockSpec((1, 1, 2, tq, LANES), lambda b, h, p, qi, kj: (b, h, 0, qi[p], 0)),
                  pl.BlockSpec((1, 1, tk, LANES), lambda b, h, p, qi, kj: (b, h, kj[p], 0)),
                  pl.BlockSpec((1, 1, tk, LANES), lambda b, h, p, qi, kj: (b, h, kj[p], 0)),
                  pl.BlockSpec((8, LANES), lambda b, h, p, qi, kj: (0, 0)),
                  pl.BlockSpec((1, LANES), lambda b, h, p, qi, kj: (0, 0))],
        out_specs=pl.BlockSpec((1, tq, LANES), lambda b, h, p, qi, kj: (b, qi[p], h)),
        scratch_shapes=[pltpu.VMEM((2 * tq, 1), F32), pltpu.VMEM((2 * tq, 2 * LANES), F32)],
    )
    return pl.pallas_call(
        kern,
        grid_spec=grid_spec,
        out_shape=jax.ShapeDtypeStruct((bsz, seq, DIFF_WIDTH), BF16),
        compiler_params=_cparams(("arbitrary", "arbitrary", "arbitrary")),
        name="flash",
    )(qi, kj, q2, kh, vh, lam_rows, subln_gain.reshape(1, LANES))


def _merge_kernel(x_ref, oa_ref, ob_ref, ga_ref, gb_ref, wa_ref, wb_ref, wo_ref, g1_ref,
                  n2_ref, sc2_ref, sh2_ref, wr_ref, br_ref,
                  x1_ref, h2_ref, idx_ref, tw_ref, *, n_experts):
    ma = _dot(oa_ref[...], wa_ref[...])
    mb = _dot(ob_ref[...], wb_ref[...])
    merged = (jax.nn.sigmoid(ga_ref[...].astype(F32)) * ma
              + jax.nn.sigmoid(gb_ref[...].astype(F32)) * mb)
    x1 = x_ref[...] + g1_ref[0] * _dot(merged.astype(BF16), wo_ref[...])
    x1_ref[...] = x1
    ms = jnp.mean(x1 * x1, axis=-1, keepdims=True)
    h2 = x1 * lax.rsqrt(ms + NORM_EPS) * n2_ref[...] * (1.0 + sc2_ref[0]) + sh2_ref[0]
    h2_ref[...] = h2

    logits = jnp.dot(h2, wr_ref[...], preferred_element_type=F32,
                     precision=lax.Precision.HIGHEST) + br_ref[...]
    lane = lax.broadcasted_iota(jnp.int32, logits.shape, 1)
    lane_f = lane.astype(F32)
    lg = jnp.where(lane < n_experts, logits, -jnp.inf)
    vals, idxs = [], []
    for _ in range(TOP_K):
        mx = jnp.max(lg, axis=-1, keepdims=True)
        ix = jnp.min(jnp.where(lg == mx, lane_f, float(LANES)), axis=-1, keepdims=True)
        vals.append(mx)
        idxs.append(ix)
        lg = jnp.where(lane_f == ix, -jnp.inf, lg)
    exps = [jnp.exp(vv - vals[0]) for vv in vals]
    denom = exps[0] + exps[1] + exps[2] + exps[3]
    idx_out = jnp.zeros(logits.shape, F32)
    tw_out = jnp.zeros(logits.shape, F32)
    for kk in range(TOP_K):
        idx_out = jnp.where(lane == kk, idxs[kk], idx_out)
        tw_out = jnp.where(lane == kk, exps[kk] / denom, tw_out)
    idx_ref[...] = idx_out.astype(jnp.int32)
    tw_ref[...] = tw_out


def _merge(x2, oa, ob, proj, wa, wb, wo, g1, n2, sc2, sh2, wr, br, seq, n_experts):
    t, d = x2.shape
    tm = min(256, seq)
    tpb = seq // tm
    gw = d
    const = lambda shape: pl.BlockSpec(shape, lambda i: tuple(0 for _ in shape))
    bvec = pl.BlockSpec((1, 1, d), lambda i: (i // tpb, 0, 0))
    kern = functools.partial(_merge_kernel, n_experts=n_experts)
    return pl.pallas_call(
        kern,
        grid=(t // tm,),
        in_specs=[pl.BlockSpec((tm, d), lambda i: (i, 0)),
                  pl.BlockSpec((tm, RWKV_WIDTH), lambda i: (i, 0)),
                  pl.BlockSpec((tm, DIFF_WIDTH), lambda i: (i, 0)),
                  pl.BlockSpec((tm, gw), lambda i: (i, OFF_GA // gw)),
                  pl.BlockSpec((tm, gw), lambda i: (i, OFF_GB // gw)),
                  const(wa.shape), const(wb.shape), const(wo.shape),
                  bvec, const((1, d)), bvec, bvec,
                  const(wr.shape), const(br.shape)],
        out_specs=[pl.BlockSpec((tm, d), lambda i: (i, 0)),
                   pl.BlockSpec((tm, d), lambda i: (i, 0)),
                   pl.BlockSpec((tm, LANES), lambda i: (i, 0)),
                   pl.BlockSpec((tm, LANES), lambda i: (i, 0))],
        out_shape=[jax.ShapeDtypeStruct((t, d), F32), jax.ShapeDtypeStruct((t, d), F32),
                   jax.ShapeDtypeStruct((t, LANES), jnp.int32),
                   jax.ShapeDtypeStruct((t, LANES), F32)],
        compiler_params=_cparams(("arbitrary",)),
        name="merge",
    )(x2, oa, ob, proj, proj, wa, wb, wo, g1, n2, sc2, sh2, wr, br)


def _row_copy(src_hbm, dst_ref, src_row, dst_row, sem):
    return pltpu.make_async_copy(src_hbm.at[pl.ds(src_row, 1)], dst_ref.at[pl.ds(dst_row, 1)], sem)


def _gather_kernel(tok_ref, h_hbm, xs_hbm, sem, *, rows):
    base = pl.program_id(0) * rows

    def issue(r, carry):
        _row_copy(h_hbm, xs_hbm, tok_ref[0, 0, r], base + r, sem).start()
        return carry

    lax.fori_loop(0, rows, issue, 0)
    block = xs_hbm.at[pl.ds(base, rows)]
    pltpu.make_async_copy(block, block, sem).wait()


def _gather_rows(h2, row_token, rows):
    p = row_token.shape[0]
    d = h2.shape[1]
    n_steps = p // rows
    kern = functools.partial(_gather_kernel, rows=rows)
    return pl.pallas_call(
        kern,
        grid=(n_steps,),
        in_specs=[pl.BlockSpec((1, 1, rows), lambda i: (i, 0, 0), memory_space=pltpu.SMEM),
                  pl.BlockSpec(memory_space=pl.ANY)],
        out_specs=pl.BlockSpec(memory_space=pl.ANY),
        out_shape=jax.ShapeDtypeStruct((p, d), h2.dtype),
        scratch_shapes=[pltpu.SemaphoreType.DMA(())],
        compiler_params=_cparams(("arbitrary",)),
        name="gather",
    )(row_token.reshape(n_steps, 1, rows), h2)


def _expert_kernel(be_ref, na_ref, xs_ref, wg_ref, bg_ref, wu_ref, bu_ref, wd_ref, bd_ref,
                   ys_ref, xb_scr, acc_scr):
    i = pl.program_id(0)
    f = pl.program_id(1)

    @pl.when(i < na_ref[0])
    def _():
        @pl.when(f == 0)
        def _():
            xb_scr[...] = xs_ref[...].astype(BF16)
            acc_scr[...] = jnp.zeros(acc_scr.shape, F32)

        xb = xb_scr[...]
        gt = jnp.minimum(_dot(xb, wg_ref[0]) + bg_ref[0], SWIGLU_LIMIT)
        up = jnp.clip(_dot(xb, wu_ref[0]) + bu_ref[0], -SWIGLU_LIMIT, SWIGLU_LIMIT)
        act = (up + 1.0) * gt * jax.nn.sigmoid(SWIGLU_ALPHA * gt)
        acc_scr[...] += _dot(act.astype(BF16), wd_ref[0])

        @pl.when(f == pl.num_programs(1) - 1)
        def _():
            ys_ref[...] = acc_scr[...] + bd_ref[0]

    @pl.when(jnp.logical_and(i >= na_ref[0], f == 0))
    def _():
        ys_ref[...] = jnp.zeros(ys_ref.shape, F32)


def _experts(xs, block_expert, n_active, wg, bg, wu, bu, wd, bd, bm):
    p, d = xs.shape
    n_exp, _, ff = wg.shape
    tf = 512
    n_blocks = p // bm

    def row(i, na):
        return jnp.minimum(i, na[0] - 1)

    grid_spec = pltpu.PrefetchScalarGridSpec(
        num_scalar_prefetch=2,
        grid=(n_blocks, ff // tf),
        in_specs=[pl.BlockSpec((bm, d), lambda i, f, be, na: (row(i, na), 0)),
                  pl.BlockSpec((1, d, tf), lambda i, f, be, na: (be[row(i, na)], 0, f)),
                  pl.BlockSpec((1, 1, tf), lambda i, f, be, na: (be[row(i, na)], 0, f)),
                  pl.BlockSpec((1, d, tf), lambda i, f, be, na: (be[row(i, na)], 0, f)),
                  pl.BlockSpec((1, 1, tf), lambda i, f, be, na: (be[row(i, na)], 0, f)),
                  pl.BlockSpec((1, tf, d), lambda i, f, be, na: (be[row(i, na)], f, 0)),
                  pl.BlockSpec((1, 1, d), lambda i, f, be, na: (be[row(i, na)], 0, 0))],
        out_specs=pl.BlockSpec((bm, d), lambda i, f, be, na: (i, 0)),
        scratch_shapes=[pltpu.VMEM((bm, d), BF16), pltpu.VMEM((bm, d), F32)],
    )
    return pl.pallas_call(
        _expert_kernel,
        grid_spec=grid_spec,
        out_shape=jax.ShapeDtypeStruct((p, d), F32),
        compiler_params=_cparams(("arbitrary", "arbitrary")),
        name="experts",
    )(block_expert, n_active, xs, wg, bg.reshape(n_exp, 1, ff), wu, bu.reshape(n_exp, 1, ff),
      wd, bd.reshape(n_exp, 1, d))


def _combine_kernel(pos_ref, x1_ref, tw_ref, g2_ref, ys_hbm, o_ref, rows_scr, sem, *, tc):
    def issue(r, carry):
        for kk in range(TOP_K):
            _row_copy(ys_hbm, rows_scr.at[kk], pos_ref[0, 0, r * TOP_K + kk], r, sem).start()
        return carry

    lax.fori_loop(0, tc, issue, 0)
    pltpu.make_async_copy(rows_scr, rows_scr, sem).wait()
    tw = tw_ref[...]
    moe = tw[:, 0:1] * rows_scr[0]
    for kk in range(1, TOP_K):
        moe = moe + tw[:, kk:kk + 1] * rows_scr[kk]
    o_ref[...] = x1_ref[...] + g2_ref[0] * moe


def _combine(x1, top_w, g2, ys, pos, seq):
    t, d = x1.shape
    tc = min(128, seq)
    tpb = seq // tc
    n_steps = t // tc
    kern = functools.partial(_combine_kernel, tc=tc)
    return pl.pallas_call(
        kern,
        grid=(n_steps,),
        in_specs=[pl.BlockSpec((1, 1, tc * TOP_K), lambda i: (i, 0, 0), memory_space=pltpu.SMEM),
                  pl.BlockSpec((tc, d), lambda i: (i, 0)),
                  pl.BlockSpec((tc, LANES), lambda i: (i, 0)),
                  pl.BlockSpec((1, 1, d), lambda i: (i // tpb, 0, 0)),
                  pl.BlockSpec(memory_space=pl.ANY)],
        out_specs=pl.BlockSpec((tc, d), lambda i: (i, 0)),
        out_shape=jax.ShapeDtypeStruct((t, d), F32),
        scratch_shapes=[pltpu.VMEM((TOP_K, tc, d), F32), pltpu.SemaphoreType.DMA(())],
        compiler_params=_cparams(("arbitrary",)),
        name="combine",
    )(pos.reshape(n_steps, 1, tc * TOP_K), x1, top_w, g2, ys)


def _route(top_idx, n_experts, bm):
    t = top_idx.shape[0]
    m = t * TOP_K
    n_blocks = (m + n_experts * (bm - 1) + bm - 1) // bm
    flat_e = top_idx.reshape(m)
    onehot = (flat_e[:, None] == jnp.arange(n_experts, dtype=jnp.int32)[None, :]).astype(jnp.int32)
    csum = jnp.cumsum(onehot, axis=0)
    rank = jnp.take_along_axis(csum, flat_e[:, None], axis=1)[:, 0] - 1
    counts = csum[-1]
    padded = (counts + bm - 1) // bm * bm
    pad_end = jnp.cumsum(padded)
    pad_start = pad_end - padded
    dest = (pad_start[flat_e] + rank).astype(jnp.int32)
    row_token = jnp.zeros((n_blocks * bm,), jnp.int32).at[dest].set(
        jnp.arange(m, dtype=jnp.int32) // TOP_K)
    block_start = jnp.arange(n_blocks, dtype=jnp.int32) * bm
    block_expert = jnp.minimum(jnp.searchsorted(pad_end, block_start, side='right'),
                               n_experts - 1).astype(jnp.int32)
    n_active = (pad_end[-1] // bm).astype(jnp.int32).reshape(1)
    return row_token, dest, block_expert, n_active


def _moe(x1, h2, top_idx, top_w, g2, wg, bg, wu, bu, wd, bd, seq):
    n_experts = wg.shape[0]
    bm = 512
    row_token, dest, block_expert, n_active = _route(top_idx[:, :TOP_K], n_experts, bm)
    xs = _gather_rows(h2, row_token, bm)
    ys = _experts(xs, block_expert, n_active, wg.astype(BF16), bg, wu.astype(BF16), bu,
                  wd.astype(BF16), bd, bm)
    return _combine(x1, top_w, g2, ys, dest, seq)


def _layer(x, c, l, w_ada, b_ada, norm1_gain, norm2_gain, w_in, shift_mu, w0, w_decay_up, a0,
           w_aaa_up, w_gate_up, k_k, k_a, r_k, gn_w, gn_b, q_gain, k_gain, lam_q1, lam_k1,
           lam_q2, lam_k2, subln_gain, w_branch_a, w_branch_b, w_out, w_router, b_router,
           w_exp_gate, b_exp_gate, w_exp_up, b_exp_up, w_exp_down, b_exp_down):
    bsz, seq, d = x.shape
    t = bsz * seq
    mod = _ada(c, w_ada, b_ada)
    sh1, sc1, g1, sh2, sc2, g2 = (m.reshape(bsz, 1, d) for m in jnp.split(mod, 6, axis=-1))

    decay_rank, aaa_rank, gate_rank = w_decay_up.shape[0], w_aaa_up.shape[0], w_gate_up.shape[0]
    w_packed, mu_packed = _pack_in_weights(w_in, shift_mu, decay_rank, aaa_rank, gate_rank)
    x2 = x.reshape(t, d)
    proj = _inproj(x2, norm1_gain.reshape(1, d), sc1, sh1, w_packed, mu_packed, seq)
    proj3 = proj.reshape(bsz, seq, PACKED_WIDTH)

    padr = lambda w: jnp.pad(w, ((0, LANES - w.shape[0]), (0, 0))).astype(BF16)
    prm = jnp.stack([w0, a0, k_k, k_a, r_k, gn_w, gn_b, jnp.zeros_like(w0)], axis=0)
    o_a = _rwkv(proj3, padr(w_decay_up), padr(w_aaa_up), w_gate_up.astype(BF16), prm, seq)

    lambda_init = 0.8 - 0.6 * math.exp(-0.3 * l)
    q2, kh, vh = _attn_prep(proj3, q_gain, k_gain, seq)
    lam_rows = jnp.pad(jnp.stack([lam_q1, lam_k1, lam_q2, lam_k2], axis=0),
                       ((0, 4), (0, LANES - DIFF_QK_DIM)))
    o_b = _flash(q2, kh, vh, lam_rows, subln_gain, seq, lambda_init)

    n_experts = w_router.shape[1]
    wr = jnp.pad(w_router, ((0, 0), (0, LANES - n_experts)))
    br = jnp.pad(b_router, (0, LANES - n_experts)).reshape(1, LANES)
    x1, h2, top_idx, top_w = _merge(
        x2, o_a.reshape(t, RWKV_WIDTH), o_b.reshape(t, DIFF_WIDTH), proj,
        w_branch_a.astype(BF16), w_branch_b.astype(BF16), w_out.astype(BF16),
        g1, norm2_gain.reshape(1, d), sc2, sh2, wr, br, seq, n_experts)

    out = _moe(x1, h2, top_idx, top_w, g2, w_exp_gate, b_exp_gate, w_exp_up, b_exp_up,
               w_exp_down, b_exp_down, seq)
    return out.reshape(bsz, seq, d)


def kernel(x, c, w_ada, b_ada, norm1_gain, norm2_gain, w_in, shift_mu, w0, w_decay_up, a0, w_aaa_up, w_gate_up, k_k, k_a, r_k, gn_w, gn_b, q_gain, k_gain, lam_q1, lam_k1, lam_q2, lam_k2, subln_gain, w_branch_a, w_branch_b, w_out, w_router, b_router, w_exp_gate, b_exp_gate, w_exp_up, b_exp_up, w_exp_down, b_exp_down):
    per_layer = (w_ada, b_ada, norm1_gain, norm2_gain, w_in, shift_mu, w0, w_decay_up, a0,
                 w_aaa_up, w_gate_up, k_k, k_a, r_k, gn_w, gn_b, q_gain, k_gain, lam_q1, lam_k1,
                 lam_q2, lam_k2, subln_gain, w_branch_a, w_branch_b, w_out, w_router, b_router,
                 w_exp_gate, b_exp_gate, w_exp_up, b_exp_up, w_exp_down, b_exp_down)
    for l in range(w_in.shape[0]):
        x = _layer(x, c, l, *(p[l] for p in per_layer))
    return x
```

```python
import functools
import math

import jax
import jax.numpy as jnp
from jax import lax
from jax.experimental import pallas as pl
from jax.experimental.pallas import tpu as pltpu

F32 = jnp.float32
BF16 = jnp.bfloat16

NORM_EPS = 1e-6
CHUNK = 64

RWKV_HEADS = 16
RWKV_HEAD_DIM = 64
RWKV_WIDTH = RWKV_HEADS * RWKV_HEAD_DIM
RWKV_GN_EPS = 64e-5

DIFF_HEADS = 8
DIFF_QK_DIM = 64
DIFF_V_DIM = 128
DIFF_WIDTH = DIFF_HEADS * DIFF_V_DIM
ROPE_THETA = 500000.0
ROPE_DIM = DIFF_QK_DIM // 4

TOP_K = 4
SWIGLU_LIMIT = 7.0
SWIGLU_ALPHA = 1.702

LANES = 128
LORA_PACK = 512
VMEM_LIMIT = 56 * 1024 * 1024

OFF_R, OFF_K, OFF_V = 0, 1024, 2048
OFF_QD, OFF_KD, OFF_VD = 3072, 4096, 5120
OFF_GA, OFF_GB = 6144, 8192
OFF_LORA = 10240
PACKED_WIDTH = OFF_LORA + LORA_PACK


def _cparams(sem):
    return pltpu.CompilerParams(dimension_semantics=sem, vmem_limit_bytes=VMEM_LIMIT)


def _nt_dot(a, b):
    return lax.dot_general(a, b, (((1,), (1,)), ((), ())), preferred_element_type=F32)


def _dot(a, b):
    return jnp.dot(a, b, preferred_element_type=F32)


def _ada_kernel(c_ref, w_ref, b_ref, o_ref):
    c = c_ref[...]
    s = c * jax.nn.sigmoid(c)
    o_ref[...] = _dot(s.astype(BF16), w_ref[...].astype(BF16)) + b_ref[...]


def _ada(c, w, b):
    bsz, d = c.shape
    n = w.shape[1]
    rows = 8
    tn = 1024
    cp = jnp.pad(c, ((0, rows - bsz), (0, 0)))
    out = pl.pallas_call(
        _ada_kernel,
        grid=(n // tn,),
        in_specs=[pl.BlockSpec((rows, d), lambda j: (0, 0)),
                  pl.BlockSpec((d, tn), lambda j: (0, j)),
                  pl.BlockSpec((1, tn), lambda j: (0, j))],
        out_specs=pl.BlockSpec((rows, tn), lambda j: (0, j)),
        out_shape=jax.ShapeDtypeStruct((rows, n), F32),
        compiler_params=_cparams(("arbitrary",)),
        name="ada",
    )(cp, w, b.reshape(1, n))
    return out[:bsz]


def _inproj_kernel(x_ref, g_ref, sc_ref, sh_ref, w_ref, mu_ref, o_ref, h_scr, carry_scr,
                   *, tiles_per_batch, n_rkv_tiles, lora_tile):
    i = pl.program_id(0)
    j = pl.program_id(1)

    @pl.when(j == 0)
    def _():
        x = x_ref[...]
        ms = jnp.mean(x * x, axis=-1, keepdims=True)
        y = x * lax.rsqrt(ms + NORM_EPS) * g_ref[...]
        h_scr[...] = (y * (1.0 + sc_ref[0]) + sh_ref[0]).astype(BF16)

    p = _dot(h_scr[...], w_ref[...])
    shifted = jnp.logical_or(j < n_rkv_tiles, j == lora_tile)

    @pl.when(shifted)
    def _():
        slot = jnp.where(j == lora_tile, n_rkv_tiles, j)

        @pl.when(i % tiles_per_batch == 0)
        def _():
            carry_scr[slot] = jnp.zeros(carry_scr.shape[1:], F32)

        prev_last = carry_scr[slot]
        tm = p.shape[0]
        carry_scr[slot] = p[tm - 1:tm, :]
        row = lax.broadcasted_iota(jnp.int32, p.shape, 0)
        prev = jnp.where(row == 0, prev_last, pltpu.roll(p, 1, 0))
        o_ref[...] = (p + (prev - p) * mu_ref[...]).astype(o_ref.dtype)

    @pl.when(jnp.logical_not(shifted))
    def _():
        o_ref[...] = p.astype(o_ref.dtype)


def _inproj(x2, gain, sc, sh, w_packed, mu_packed, seq):
    t, d = x2.shape
    tm = min(1024, seq)
    tn = 512
    n_col = PACKED_WIDTH // tn
    n_rkv_tiles = OFF_QD // tn
    lora_tile = OFF_LORA // tn
    tiles_per_batch = seq // tm

    def mu_map(i, j):
        return (0, jnp.where(j == lora_tile, n_rkv_tiles, jnp.minimum(j, n_rkv_tiles - 1)))

    kern = functools.partial(_inproj_kernel, tiles_per_batch=tiles_per_batch,
                             n_rkv_tiles=n_rkv_tiles, lora_tile=lora_tile)
    return pl.pallas_call(
        kern,
        grid=(t // tm, n_col),
        in_specs=[pl.BlockSpec((tm, d), lambda i, j: (i, 0)),
                  pl.BlockSpec((1, d), lambda i, j: (0, 0)),
                  pl.BlockSpec((1, 1, d), lambda i, j: (i // tiles_per_batch, 0, 0)),
                  pl.BlockSpec((1, 1, d), lambda i, j: (i // tiles_per_batch, 0, 0)),
                  pl.BlockSpec((d, tn), lambda i, j: (0, j)),
                  pl.BlockSpec((1, tn), mu_map)],
        out_specs=pl.BlockSpec((tm, tn), lambda i, j: (i, j)),
        out_shape=jax.ShapeDtypeStruct((t, PACKED_WIDTH), BF16),
        scratch_shapes=[pltpu.VMEM((tm, d), BF16),
                        pltpu.VMEM((n_rkv_tiles + 1, 1, tn), F32)],
        compiler_params=_cparams(("arbitrary", "arbitrary")),
        name="inproj",
    )(x2, gain, sc, sh, w_packed, mu_packed)


def _pack_in_weights(w_in, shift_mu, decay_rank, aaa_rank, gate_rank):
    assert decay_rank <= LANES and aaa_rank <= LANES and gate_rank == 2 * LANES
    rw = RWKV_WIDTH
    o = 3 * rw
    w_r, w_k, w_v = w_in[:, 0:rw], w_in[:, rw:2 * rw], w_in[:, 2 * rw:3 * rw]
    w_dl = w_in[:, o:o + decay_rank]
    w_al = w_in[:, o + decay_rank:o + decay_rank + aaa_rank]
    w_gl = w_in[:, o + decay_rank + aaa_rank:o + decay_rank + aaa_rank + gate_rank]
    rest = w_in[:, o + decay_rank + aaa_rank + gate_rank:]

    def padc(a, width):
        return jnp.pad(a, ((0, 0), (0, width - a.shape[1])))

    packed = jnp.concatenate(
        [w_r, w_k, w_v, rest, padc(w_dl, LANES), padc(w_al, LANES), w_gl], axis=1).astype(BF16)
    assert packed.shape[1] == PACKED_WIDTH
    mu = shift_mu.reshape(1, -1)
    mu_packed = jnp.concatenate(
        [mu[:, :o], padc(mu[:, o:o + decay_rank], LANES),
         padc(mu[:, o + decay_rank:o + decay_rank + aaa_rank], LANES),
         mu[:, o + decay_rank + aaa_rank:]], axis=1)
    return packed, mu_packed


def _split3_bf16(x):
    hi = x.astype(BF16)
    r1 = x - hi.astype(F32)
    mid = r1.astype(BF16)
    lo = (r1 - mid.astype(F32)).astype(BF16)
    return hi, mid, lo


def _rwkv_kernel(r_ref, k_ref, v_ref, lora_ref, wd_ref, wa_ref, wg_ref, prm_ref, o_ref, st_ref,
                 *, tm):
    c = CHUNK
    n = RWKV_HEAD_DIM
    n_chunks = tm // c

    @pl.when(pl.program_id(2) == 0)
    def _():
        st_ref[...] = jnp.zeros(st_ref.shape, F32)

    lane = lax.broadcasted_iota(jnp.int32, (1, LANES), 1)
    m0 = lane < n
    m0_256 = jnp.concatenate([m0, m0], axis=1)

    def head_sum(x):
        s0 = jnp.sum(jnp.where(m0, x, 0.0), axis=-1, keepdims=True)
        s1 = jnp.sum(jnp.where(m0, 0.0, x), axis=-1, keepdims=True)
        return jnp.where(m0, s0, s1)

    prm = prm_ref[...]
    w0, a0, k_k, k_a, r_k, gn_w, gn_b = (prm[i:i + 1] for i in range(7))

    r = r_ref[0].astype(F32)
    k = k_ref[0].astype(F32)
    v = v_ref[0].astype(F32)
    lora = lora_ref[0]
    d_code = lora[:, 0:LANES].astype(F32)
    a_code = lora[:, LANES:2 * LANES]
    g_code = lora[:, 2 * LANES:4 * LANES].astype(F32)

    wl = w0 + _dot(jnp.tanh(d_code).astype(BF16), wd_ref[...])
    z = -wl
    softplus = jnp.maximum(z, 0.0) + jnp.log(1.0 + jnp.exp(-jnp.abs(z)))
    lw = -jnp.exp(-softplus - 0.5)
    a = jax.nn.sigmoid(a0 + _dot(a_code, wa_ref[...]))
    g = _dot(jax.nn.sigmoid(g_code).astype(BF16), wg_ref[...])

    kk = k * k_k
    kk = kk / jnp.maximum(jnp.sqrt(head_sum(kk * kk)), 1e-12)
    k2 = k * (1.0 + (a - 1.0) * k_a)
    a_vec = -kk
    b_vec = kk * a
    bonus = head_sum(r * k2 * r_k) * v

    tb = min(tm, 2 * LANES)
    ri = lax.broadcasted_iota(jnp.int32, (tb, tb), 0)
    ci = lax.broadcasted_iota(jnp.int32, (tb, tb), 1)
    tri = jnp.where(jnp.logical_and(ci <= ri, ci // c == ri // c), 1.0, 0.0).astype(BF16)
    lw3 = jnp.concatenate(_split3_bf16(lw), axis=1)
    cum3 = jnp.concatenate([_dot(tri, lw3[u * tb:(u + 1) * tb]) for u in range(tm // tb)], axis=0)
    cum = cum3[:, 0:LANES] + cum3[:, LANES:2 * LANES] + cum3[:, 2 * LANES:3 * LANES]
    cum_end = jnp.concatenate(
        [jnp.broadcast_to(cum[(q + 1) * c - 1:(q + 1) * c, :], (c, LANES)) for q in range(n_chunks)],
        axis=0)

    e_inc = jnp.exp(cum)
    e_exc = jnp.exp(cum - lw)
    e_neg = jnp.exp(-cum)
    e_end = jnp.exp(cum_end - cum)
    a_t = a_vec * e_exc
    r_t = r * e_inc
    k_h = k2 * e_neg
    b_h = b_vec * e_neg
    k_d = k2 * e_end
    b_d = b_vec * e_end
    d_end = jnp.exp(cum_end)

    row64 = lax.broadcasted_iota(jnp.int32, (c, LANES), 0)
    col64 = lax.broadcasted_iota(jnp.int32, (c, LANES), 1) % n
    strict = row64 > col64
    incl = row64 >= col64
    rb = lax.broadcasted_iota(jnp.int32, (LANES, LANES), 0) < n
    cb = lax.broadcasted_iota(jnp.int32, (LANES, LANES), 1) < n
    blockdiag = rb == cb

    def stack_heads(zz, mask):
        return jnp.concatenate([jnp.where(mask, zz, 0.0), jnp.where(mask, 0.0, zz)],
                               axis=0).astype(BF16)

    qs = range(n_chunks)
    ch = lambda t: [t[q * c:(q + 1) * c] for q in qs]
    at_c, rt_c, kh_c, bh_c, kd_c, bd_c, v_c = (ch(t) for t in (a_t, r_t, k_h, b_h, k_d, b_d, v))
    out0 = [_nt_dot(jnp.concatenate([jnp.where(m0, at_c[q], 0.0), jnp.where(m0, rt_c[q], 0.0)],
                                    axis=0).astype(BF16),
                    jnp.concatenate([bh_c[q], kh_c[q]], axis=0).astype(BF16)) for q in qs]
    out1 = [_nt_dot(jnp.concatenate([jnp.where(m0, 0.0, at_c[q]), jnp.where(m0, 0.0, rt_c[q])],
                                    axis=0).astype(BF16),
                    jnp.concatenate([kh_c[q], bh_c[q]], axis=0).astype(BF16)) for q in qs]
    x_cat = [jnp.where(strict, jnp.where(m0, out0[q][:c], out1[q][:c]), 0.0) for q in qs]
    ak_x = [jnp.where(strict, jnp.where(m0, out1[q][:c], out0[q][:c]), 0.0) for q in qs]
    rb_cat = [jnp.where(incl, jnp.where(m0, out0[q][c:], out1[q][c:]), 0.0) for q in qs]
    rk_x = [jnp.where(incl, jnp.where(m0, out1[q][c:], out0[q][c:]), 0.0) for q in qs]
    w12 = [_dot(jnp.concatenate([ak_x[q], rk_x[q]], axis=0).astype(BF16),
                jnp.concatenate([jnp.where(m0, 0.0, v_c[q]), jnp.where(m0, v_c[q], 0.0)],
                                axis=0).astype(BF16)) for q in qs]

    zz = [jnp.concatenate([at_c[q], w12[q][:c]], axis=1) for q in qs]
    xp = x_cat
    for step in range(6):
        zz = [zz[q] + _dot(xp[q].astype(BF16), stack_heads(zz[q], m0_256)) for q in qs]
        if step < 5:
            xp = [_dot(xp[q].astype(BF16), stack_heads(xp[q], m0)) for q in qs]
    rbz = [_dot(rb_cat[q].astype(BF16), stack_heads(zz[q], m0_256)) for q in qs]
    q_c = [rt_c[q] + rbz[q][:, 0:LANES] for q in qs]
    z_c = [w12[q][c:] + rbz[q][:, LANES:] for q in qs]
    mn = []
    for q in qs:
        t_src = jnp.concatenate(
            [zz[q], jnp.concatenate([jnp.zeros_like(v_c[q]), v_c[q]], axis=1)], axis=0)
        mn.append(_dot(t_src.T.astype(BF16),
                       jnp.concatenate([bd_c[q], kd_c[q]], axis=0).astype(BF16)))
    m_x = [jnp.where(blockdiag, mn[q][0:LANES], 0.0).astype(BF16) for q in qs]
    n_x = [jnp.where(blockdiag, mn[q][LANES:], 0.0) for q in qs]

    state = st_ref[...]
    ys = []
    for q in qs:
        s_bf = state.astype(BF16)
        ys.append(_nt_dot(q_c[q].astype(BF16), s_bf) + z_c[q])
        state = state * d_end[q * c:q * c + 1, :] + _dot(s_bf, m_x[q]) + n_x[q]

    st_ref[...] = state
    y = jnp.concatenate(ys, axis=0)
    mu = head_sum(y) * (1.0 / n)
    yc = y - mu
    var = head_sum(yc * yc) * (1.0 / n)
    yn = yc * lax.rsqrt(var + RWKV_GN_EPS) * gn_w + gn_b
    o_ref[0] = ((yn + bonus) * g).astype(o_ref.dtype)


def _rwkv(proj3, wd, wa, wg, prm, seq):
    bsz = proj3.shape[0]
    tm = min(512, seq)
    n_pairs = RWKV_WIDTH // LANES
    blk = lambda off: (lambda b, p, t: (b, t, off // LANES + p))
    kern = functools.partial(_rwkv_kernel, tm=tm)
    return pl.pallas_call(
        kern,
        grid=(bsz, n_pairs, seq // tm),
        in_specs=[pl.BlockSpec((1, tm, LANES), blk(OFF_R)),
                  pl.BlockSpec((1, tm, LANES), blk(OFF_K)),
                  pl.BlockSpec((1, tm, LANES), blk(OFF_V)),
                  pl.BlockSpec((1, tm, LORA_PACK), lambda b, p, t: (b, t, OFF_LORA // LORA_PACK)),
                  pl.BlockSpec((LANES, LANES), lambda b, p, t: (0, p)),
                  pl.BlockSpec((LANES, LANES), lambda b, p, t: (0, p)),
                  pl.BlockSpec((2 * LANES, LANES), lambda b, p, t: (0, p)),
                  pl.BlockSpec((8, LANES), lambda b, p, t: (0, p))],
        out_specs=pl.BlockSpec((1, tm, LANES), lambda b, p, t: (b, t, p)),
        out_shape=jax.ShapeDtypeStruct((bsz, seq, RWKV_WIDTH), BF16),
        scratch_shapes=[pltpu.VMEM((LANES, LANES), F32)],
        compiler_params=_cparams(("arbitrary", "arbitrary", "arbitrary")),
        name="rwkv",
    )(proj3, proj3, proj3, proj3, wd, wa, wg, prm)


def _attn_prep_kernel(q_ref, k_ref, v_ref, cos_ref, sa_ref, sb_ref, gavg_ref, qg_ref, kg_ref,
                      q2_ref, ko_ref, vo_ref):
    width = q_ref.shape[-1]
    reps = width // LANES
    tile = lambda t: jnp.concatenate([t] * reps, axis=1)
    cos_t, sin_a, sin_b = tile(cos_ref[...]), tile(sa_ref[...]), tile(sb_ref[...])
    gavg = gavg_ref[...]
    half = ROPE_DIM // 2

    def norm_rope(x, gain):
        x = x.astype(F32)
        sq = x * x
        hi = sq.astype(BF16)
        lo = (sq - hi.astype(F32)).astype(BF16)
        ms = _dot(hi, gavg) + _dot(lo, gavg)
        y = x * lax.rsqrt(ms + NORM_EPS) * gain
        return (y * cos_t + pltpu.roll(y, width - half, 1) * sin_a + pltpu.roll(y, half, 1) * sin_b)

    qn = norm_rope(q_ref[0], qg_ref[...])
    kn = norm_rope(k_ref[0], kg_ref[...])
    lane = lax.broadcasted_iota(jnp.int32, (1, LANES), 1)
    m0 = lane < DIFF_QK_DIM
    for h in range(DIFF_HEADS):
        sl = slice(h * LANES, (h + 1) * LANES)
        qh = qn[:, sl]
        q2_ref[0, h, 0] = jnp.where(m0, qh, 0.0).astype(BF16)
        q2_ref[0, h, 1] = jnp.where(m0, 0.0, qh).astype(BF16)
        ko_ref[0, h] = kn[:, sl].astype(BF16)
        vo_ref[0, h] = v_ref[0, :, sl]


def _rope_tables(seq):
    half = ROPE_DIM // 2
    inv_freq = ROPE_THETA ** (-jnp.arange(half, dtype=F32) * 2.0 / ROPE_DIM)
    ang = jnp.arange(seq, dtype=jnp.int32).astype(F32)[:, None] * inv_freq[None, :]
    cos, sin = jnp.cos(ang), jnp.sin(ang)
    ones = jnp.ones((seq, DIFF_QK_DIM - ROPE_DIM), F32)
    zeros = jnp.zeros((seq, DIFF_QK_DIM - ROPE_DIM), F32)
    zh = jnp.zeros((seq, half), F32)
    cos64 = jnp.concatenate([cos, cos, ones], axis=1)
    sa64 = jnp.concatenate([-sin, zh, zeros], axis=1)
    sb64 = jnp.concatenate([zh, sin, zeros], axis=1)
    dup = lambda t: jnp.concatenate([t, t], axis=1)
    return dup(cos64), dup(sa64), dup(sb64)


def _attn_prep(proj3, q_gain, k_gain, seq):
    bsz = proj3.shape[0]
    tm = min(512, seq)
    width = DIFF_HEADS * 2 * DIFF_QK_DIM
    cos_t, sin_a, sin_b = _rope_tables(seq)
    gi = jnp.arange(width) // DIFF_QK_DIM
    gavg = jnp.where(gi[:, None] == gi[None, :], 1.0 / DIFF_QK_DIM, 0.0).astype(BF16)
    reps = width // DIFF_QK_DIM
    qg = (jnp.tile(q_gain, reps) * (DIFF_QK_DIM ** -0.5 * math.log2(math.e))).reshape(1, width)
    kg = jnp.tile(k_gain, reps).reshape(1, width)
    hd = (bsz, DIFF_HEADS, seq, LANES)
    tab = pl.BlockSpec((tm, LANES), lambda b, t: (t, 0))
    return pl.pallas_call(
        _attn_prep_kernel,
        grid=(bsz, seq // tm),
        in_specs=[pl.BlockSpec((1, tm, width), lambda b, t: (b, t, OFF_QD // width)),
                  pl.BlockSpec((1, tm, width), lambda b, t: (b, t, OFF_KD // width)),
                  pl.BlockSpec((1, tm, width), lambda b, t: (b, t, OFF_VD // width)),
                  tab, tab, tab,
                  pl.BlockSpec((width, width), lambda b, t: (0, 0)),
                  pl.BlockSpec((1, width), lambda b, t: (0, 0)),
                  pl.BlockSpec((1, width), lambda b, t: (0, 0))],
        out_specs=[pl.BlockSpec((1, DIFF_HEADS, 2, tm, LANES), lambda b, t: (b, 0, 0, t, 0)),
                   pl.BlockSpec((1, DIFF_HEADS, tm, LANES), lambda b, t: (b, 0, t, 0)),
                   pl.BlockSpec((1, DIFF_HEADS, tm, LANES), lambda b, t: (b, 0, t, 0))],
        out_shape=[jax.ShapeDtypeStruct((bsz, DIFF_HEADS, 2, seq, LANES), BF16),
                   jax.ShapeDtypeStruct(hd, BF16), jax.ShapeDtypeStruct(hd, BF16)],
        compiler_params=_cparams(("arbitrary", "arbitrary")),
        name="attn_prep",
    )(proj3, proj3, proj3, cos_t, sin_a, sin_b, gavg, qg, kg)


def _flash_kernel(qi_ref, kj_ref, q_ref, k_ref, v_ref, lam_ref, sg_ref, o_ref, m_ref, acc_ref,
                  *, tq, tk, lambda_init):
    pidx = pl.program_id(2)
    i = qi_ref[pidx]
    j = kj_ref[pidx]

    @pl.when(j == 0)
    def _():
        m_ref[...] = jnp.full(m_ref.shape, -jnp.inf, F32)
        acc_ref[...] = jnp.zeros(acc_ref.shape, F32)

    q = q_ref[0, 0].reshape(2 * tq, LANES)
    s = _nt_dot(q, k_ref[0, 0])

    def update(sc):
        m_old = m_ref[...]
        m_new = jnp.maximum(m_old, jnp.max(sc, axis=-1, keepdims=True))
        alpha = jnp.exp2(m_old - m_new)
        p = jnp.exp2(sc - jnp.concatenate([m_new] * (tk // LANES), axis=1)).astype(BF16)
        v = v_ref[0, 0]
        v_ext = jnp.concatenate([v, jnp.ones_like(v)], axis=1)
        acc_ref[...] = jnp.concatenate([alpha, alpha], axis=1) * acc_ref[...] + _dot(p, v_ext)
        m_ref[...] = m_new

    @pl.when(j < i)
    def _():
        update(s)

    @pl.when(j == i)
    def _():
        qpos = lax.broadcasted_iota(jnp.int32, s.shape, 0) % tq
        kpos = lax.broadcasted_iota(jnp.int32, s.shape, 1)
        update(jnp.where(kpos // CHUNK <= qpos // CHUNK, s, -jnp.inf))
        lam_rows = lam_ref[...]
        lam = (jnp.exp(jnp.sum(lam_rows[0:1] * lam_rows[1:2], axis=-1, keepdims=True))
               - jnp.exp(jnp.sum(lam_rows[2:3] * lam_rows[3:4], axis=-1, keepdims=True))
               + lambda_init)
        acc = acc_ref[...]
        o1 = acc[0:tq, 0:LANES] / acc[0:tq, LANES:]
        o2 = acc[tq:, 0:LANES] / acc[tq:, LANES:]
        o = o1 - lam * o2
        ms = jnp.mean(o * o, axis=-1, keepdims=True)
        o_ref[0] = (o * lax.rsqrt(ms + NORM_EPS) * sg_ref[...] * (1.0 - lambda_init)).astype(o_ref.dtype)


def _flash(q2, kh, vh, lam_rows, subln_gain, seq, lambda_init):
    bsz = q2.shape[0]
    tq = tk = min(1024, seq)
    nq = seq // tq
    pairs = [(i, j) for i in range(nq) for j in range(i + 1)]
    qi = jnp.asarray([p[0] for p in pairs], jnp.int32)
    kj = jnp.asarray([p[1] for p in pairs], jnp.int32)
    kern = functools.partial(_flash_kernel, tq=tq, tk=tk, lambda_init=lambda_init)
    grid_spec = pltpu.PrefetchScalarGridSpec(
        num_scalar_prefetch=2,
        grid=(bsz, DIFF_HEADS, len(pairs)),
        in_specs=[pl.BlockSpec((1, 1, 2, tq, LANES), lambda b, h, p, qi, kj: (b, h, 0, qi[p], 0)),
                  pl.BlockSpec((1, 1, tk, LANES), lambda b, h, p, qi, kj: (b, h, kj[p], 0)),
                  pl.BlockSpec((1, 1, tk, LANES), lambda b, h, p, qi, kj: (b, h, kj[p], 0)),
                  pl.BlockSpec((8, LANES), lambda b, h, p, qi, kj: (0, 0)),
                  pl.BlockSpec((1, LANES), lambda b, h, p, qi, kj: (0, 0))],
        out_specs=pl.BlockSpec((1, tq, LANES), lambda b, h, p, qi, kj: (b, qi[p], h)),
        scratch_shapes=[pltpu.VMEM((2 * tq, LANES), F32), pltpu.VMEM((2 * tq, 2 * LANES), F32)],
    )
    return pl.pallas_call(
        kern,
        grid_spec=grid_spec,
        out_shape=jax.ShapeDtypeStruct((bsz, seq, DIFF_WIDTH), BF16),
        compiler_params=_cparams(("arbitrary", "arbitrary", "arbitrary")),
        name="flash",
    )(qi, kj, q2, kh, vh, lam_rows, subln_gain.reshape(1, LANES))


def _merge_kernel(x_ref, oa_ref, ob_ref, ga_ref, gb_ref, wa_ref, wb_ref, wo_ref, g1_ref,
                  n2_ref, sc2_ref, sh2_ref, wr_ref, br_ref,
                  x1_ref, h2_ref, idx_ref, tw_ref, *, n_experts):
    ma = _dot(oa_ref[...], wa_ref[...])
    mb = _dot(ob_ref[...], wb_ref[...])
    merged = (jax.nn.sigmoid(ga_ref[...].astype(F32)) * ma
              + jax.nn.sigmoid(gb_ref[...].astype(F32)) * mb)
    x1 = x_ref[...] + g1_ref[0] * _dot(merged.astype(BF16), wo_ref[...])
    x1_ref[...] = x1
    ms = jnp.mean(x1 * x1, axis=-1, keepdims=True)
    h2 = x1 * lax.rsqrt(ms + NORM_EPS) * n2_ref[...] * (1.0 + sc2_ref[0]) + sh2_ref[0]
    h2_ref[...] = h2

    logits = jnp.dot(h2, wr_ref[...], preferred_element_type=F32,
                     precision=lax.Precision.HIGHEST) + br_ref[...]
    lane = lax.broadcasted_iota(jnp.int32, logits.shape, 1)
    lane_f = lane.astype(F32)
    lg = jnp.where(lane < n_experts, logits, -jnp.inf)
    vals, idxs = [], []
    for _ in range(TOP_K):
        mx = jnp.max(lg, axis=-1, keepdims=True)
        ix = jnp.min(jnp.where(lg == mx, lane_f, float(LANES)), axis=-1, keepdims=True)
        vals.append(mx)
        idxs.append(ix)
        lg = jnp.where(lane_f == ix, -jnp.inf, lg)
    exps = [jnp.exp(vv - vals[0]) for vv in vals]
    denom = exps[0] + exps[1] + exps[2] + exps[3]
    idx_out = jnp.zeros(logits.shape, F32)
    tw_out = jnp.zeros(logits.shape, F32)
    for kk in range(TOP_K):
        idx_out = jnp.where(lane == kk, idxs[kk], idx_out)
        tw_out = jnp.where(lane == kk, exps[kk] / denom, tw_out)
    idx_ref[...] = idx_out.astype(jnp.int32)
    tw_ref[...] = tw_out


def _merge(x2, oa, ob, proj, wa, wb, wo, g1, n2, sc2, sh2, wr, br, seq, n_experts):
    t, d = x2.shape
    tm = min(256, seq)
    tpb = seq // tm
    gw = d
    const = lambda shape: pl.BlockSpec(shape, lambda i: tuple(0 for _ in shape))
    bvec = pl.BlockSpec((1, 1, d), lambda i: (i // tpb, 0, 0))
    kern = functools.partial(_merge_kernel, n_experts=n_experts)
    return pl.pallas_call(
        kern,
        grid=(t // tm,),
        in_specs=[pl.BlockSpec((tm, d), lambda i: (i, 0)),
                  pl.BlockSpec((tm, RWKV_WIDTH), lambda i: (i, 0)),
                  pl.BlockSpec((tm, DIFF_WIDTH), lambda i: (i, 0)),
                  pl.BlockSpec((tm, gw), lambda i: (i, OFF_GA // gw)),
                  pl.BlockSpec((tm, gw), lambda i: (i, OFF_GB // gw)),
                  const(wa.shape), const(wb.shape), const(wo.shape),
                  bvec, const((1, d)), bvec, bvec,
                  const(wr.shape), const(br.shape)],
        out_specs=[pl.BlockSpec((tm, d), lambda i: (i, 0)),
                   pl.BlockSpec((tm, d), lambda i: (i, 0)),
                   pl.BlockSpec((tm, LANES), lambda i: (i, 0)),
                   pl.BlockSpec((tm, LANES), lambda i: (i, 0))],
        out_shape=[jax.ShapeDtypeStruct((t, d), F32), jax.ShapeDtypeStruct((t, d), F32),
                   jax.ShapeDtypeStruct((t, LANES), jnp.int32),
                   jax.ShapeDtypeStruct((t, LANES), F32)],
        compiler_params=_cparams(("arbitrary",)),
        name="merge",
    )(x2, oa, ob, proj, proj, wa, wb, wo, g1, n2, sc2, sh2, wr, br)


def _row_copy(src_hbm, dst_ref, src_row, dst_row, sem):
    return pltpu.make_async_copy(src_hbm.at[pl.ds(src_row, 1)], dst_ref.at[pl.ds(dst_row, 1)], sem)


def _expert_kernel(be_ref, na_ref, tok0_ref, tokn_ref, h_hbm, wg_ref, bg_ref, wu_ref, bu_ref,
                   wd_ref, bd_ref, ys_ref, xs_scr, xb_scr, acc_scr, sems, *, bm, rows_per_step):
    i = pl.program_id(0)
    f = pl.program_id(1)
    n_act = na_ref[0]
    slot = i % 2

    def wait_slot(s):
        pltpu.make_async_copy(xs_scr.at[s], xs_scr.at[s], sems.at[s]).wait()

    @pl.when(i < n_act)
    def _():
        @pl.when(f == 0)
        def _():
            @pl.when(i == 0)
            def _():
                def issue(r, carry):
                    _row_copy(h_hbm, xs_scr.at[0], tok0_ref[0, 0, r], r, sems.at[0]).start()
                    return carry

                lax.fori_loop(0, bm, issue, 0)

            wait_slot(slot)
            xb_scr[...] = xs_scr[slot].astype(BF16)
            acc_scr[...] = jnp.zeros(acc_scr.shape, F32)

        for r in range(rows_per_step):
            _row_copy(h_hbm, xs_scr.at[1 - slot], tokn_ref[0, 0, r], f * rows_per_step + r,
                      sems.at[1 - slot]).start()

        xb = xb_scr[...]
        gt = jnp.minimum(_dot(xb, wg_ref[0]) + bg_ref[0], SWIGLU_LIMIT)
        up = jnp.clip(_dot(xb, wu_ref[0]) + bu_ref[0], -SWIGLU_LIMIT, SWIGLU_LIMIT)
        act = (up + 1.0) * gt * jax.nn.sigmoid(SWIGLU_ALPHA * gt)
        acc_scr[...] += _dot(act.astype(BF16), wd_ref[0])

        @pl.when(f == pl.num_programs(1) - 1)
        def _():
            ys_ref[...] = acc_scr[...] + bd_ref[0]

            @pl.when(i == n_act - 1)
            def _():
                wait_slot(1 - slot)

    @pl.when(jnp.logical_and(i >= n_act, f == 0))
    def _():
        ys_ref[...] = jnp.zeros(ys_ref.shape, F32)


def _experts(h2, row_token, block_expert, n_active, wg, bg, wu, bu, wd, bd, bm):
    d = h2.shape[1]
    n_exp, _, ff = wg.shape
    tf = 512
    nf = ff // tf
    rows_per_step = bm // nf
    p = row_token.shape[0]
    n_blocks = p // bm

    def row(i, na):
        return jnp.minimum(i, na[0] - 1)

    def next_rows(i, f, be, na):
        return (jnp.minimum(i + 1, n_blocks - 1) * nf + f, 0, 0)

    kern = functools.partial(_expert_kernel, bm=bm, rows_per_step=rows_per_step)
    grid_spec = pltpu.PrefetchScalarGridSpec(
        num_scalar_prefetch=2,
        grid=(n_blocks, nf),
        in_specs=[pl.BlockSpec((1, 1, bm), lambda i, f, be, na: (0, 0, 0), memory_space=pltpu.SMEM),
                  pl.BlockSpec((1, 1, rows_per_step), next_rows, memory_space=pltpu.SMEM),
                  pl.BlockSpec(memory_space=pl.ANY),
                  pl.BlockSpec((1, d, tf), lambda i, f, be, na: (be[row(i, na)], 0, f)),
                  pl.BlockSpec((1, 1, tf), lambda i, f, be, na: (be[row(i, na)], 0, f)),
                  pl.BlockSpec((1, d, tf), lambda i, f, be, na: (be[row(i, na)], 0, f)),
                  pl.BlockSpec((1, 1, tf), lambda i, f, be, na: (be[row(i, na)], 0, f)),
                  pl.BlockSpec((1, tf, d), lambda i, f, be, na: (be[row(i, na)], f, 0)),
                  pl.BlockSpec((1, 1, d), lambda i, f, be, na: (be[row(i, na)], 0, 0))],
        out_specs=pl.BlockSpec((bm, d), lambda i, f, be, na: (i, 0)),
        scratch_shapes=[pltpu.VMEM((2, bm, d), F32), pltpu.VMEM((bm, d), BF16),
                        pltpu.VMEM((bm, d), F32), pltpu.SemaphoreType.DMA((2,))],
    )
    return pl.pallas_call(
        kern,
        grid_spec=grid_spec,
        out_shape=jax.ShapeDtypeStruct((p, d), F32),
        compiler_params=_cparams(("arbitrary", "arbitrary")),
        name="experts",
    )(block_expert, n_active, row_token.reshape(n_blocks, 1, bm),
      row_token.reshape(n_blocks * nf, 1, rows_per_step), h2,
      wg, bg.reshape(n_exp, 1, ff), wu, bu.reshape(n_exp, 1, ff), wd, bd.reshape(n_exp, 1, d))


def _combine_kernel(pos_ref, x1_ref, tw_ref, g2_ref, ys_hbm, o_ref, rows_scr, sem, *, tc):
    def issue(r, carry):
        for kk in range(TOP_K):
            _row_copy(ys_hbm, rows_scr.at[kk], pos_ref[0, 0, r * TOP_K + kk], r, sem).start()
        return carry

    lax.fori_loop(0, tc, issue, 0)
    pltpu.make_async_copy(rows_scr, rows_scr, sem).wait()
    tw = tw_ref[...]
    moe = tw[:, 0:1] * rows_scr[0]
    for kk in range(1, TOP_K):
        moe = moe + tw[:, kk:kk + 1] * rows_scr[kk]
    o_ref[...] = x1_ref[...] + g2_ref[0] * moe


def _combine(x1, top_w, g2, ys, pos, seq):
    t, d = x1.shape
    tc = min(128, seq)
    tpb = seq // tc
    n_steps = t // tc
    kern = functools.partial(_combine_kernel, tc=tc)
    return pl.pallas_call(
        kern,
        grid=(n_steps,),
        in_specs=[pl.BlockSpec((1, 1, tc * TOP_K), lambda i: (i, 0, 0), memory_space=pltpu.SMEM),
                  pl.BlockSpec((tc, d), lambda i: (i, 0)),
                  pl.BlockSpec((tc, LANES), lambda i: (i, 0)),
                  pl.BlockSpec((1, 1, d), lambda i: (i // tpb, 0, 0)),
                  pl.BlockSpec(memory_space=pl.ANY)],
        out_specs=pl.BlockSpec((tc, d), lambda i: (i, 0)),
        out_shape=jax.ShapeDtypeStruct((t, d), F32),
        scratch_shapes=[pltpu.VMEM((TOP_K, tc, d), F32), pltpu.SemaphoreType.DMA(())],
        compiler_params=_cparams(("arbitrary",)),
        name="combine",
    )(pos.reshape(n_steps, 1, tc * TOP_K), x1, top_w, g2, ys)


def _route(top_idx, n_experts, bm):
    t = top_idx.shape[0]
    m = t * TOP_K
    n_blocks = (m + n_experts * (bm - 1) + bm - 1) // bm
    flat_e = top_idx.reshape(m)
    onehot = (flat_e[:, None] == jnp.arange(n_experts, dtype=jnp.int32)[None, :]).astype(jnp.int32)
    csum = jnp.cumsum(onehot, axis=0)
    rank = jnp.take_along_axis(csum, flat_e[:, None], axis=1)[:, 0] - 1
    counts = csum[-1]
    padded = (counts + bm - 1) // bm * bm
    pad_end = jnp.cumsum(padded)
    pad_start = pad_end - padded
    dest = (pad_start[flat_e] + rank).astype(jnp.int32)
    row_token = jnp.zeros((n_blocks * bm,), jnp.int32).at[dest].set(
        jnp.arange(m, dtype=jnp.int32) // TOP_K)
    block_start = jnp.arange(n_blocks, dtype=jnp.int32) * bm
    block_expert = jnp.minimum(jnp.searchsorted(pad_end, block_start, side='right'),
                               n_experts - 1).astype(jnp.int32)
    n_active = (pad_end[-1] // bm).astype(jnp.int32).reshape(1)
    return row_token, dest, block_expert, n_active


def _moe(x1, h2, top_idx, top_w, g2, wg, bg, wu, bu, wd, bd, seq):
    n_experts = wg.shape[0]
    bm = 512
    row_token, dest, block_expert, n_active = _route(top_idx[:, :TOP_K], n_experts, bm)
    ys = _experts(h2, row_token, block_expert, n_active, wg.astype(BF16), bg, wu.astype(BF16), bu,
                  wd.astype(BF16), bd, bm)
    return _combine(x1, top_w, g2, ys, dest, seq)


def _layer(x, c, l, w_ada, b_ada, norm1_gain, norm2_gain, w_in, shift_mu, w0, w_decay_up, a0,
           w_aaa_up, w_gate_up, k_k, k_a, r_k, gn_w, gn_b, q_gain, k_gain, lam_q1, lam_k1,
           lam_q2, lam_k2, subln_gain, w_branch_a, w_branch_b, w_out, w_router, b_router,
           w_exp_gate, b_exp_gate, w_exp_up, b_exp_up, w_exp_down, b_exp_down):
    bsz, seq, d = x.shape
    t = bsz * seq
    mod = _ada(c, w_ada, b_ada)
    sh1, sc1, g1, sh2, sc2, g2 = (m.reshape(bsz, 1, d) for m in jnp.split(mod, 6, axis=-1))

    decay_rank, aaa_rank, gate_rank = w_decay_up.shape[0], w_aaa_up.shape[0], w_gate_up.shape[0]
    w_packed, mu_packed = _pack_in_weights(w_in, shift_mu, decay_rank, aaa_rank, gate_rank)
    x2 = x.reshape(t, d)
    proj = _inproj(x2, norm1_gain.reshape(1, d), sc1, sh1, w_packed, mu_packed, seq)
    proj3 = proj.reshape(bsz, seq, PACKED_WIDTH)

    padr = lambda w: jnp.pad(w, ((0, LANES - w.shape[0]), (0, 0))).astype(BF16)
    prm = jnp.stack([w0, a0, k_k, k_a, r_k, gn_w, gn_b, jnp.zeros_like(w0)], axis=0)
    o_a = _rwkv(proj3, padr(w_decay_up), padr(w_aaa_up), w_gate_up.astype(BF16), prm, seq)

    lambda_init = 0.8 - 0.6 * math.exp(-0.3 * l)
    q2, kh, vh = _attn_prep(proj3, q_gain, k_gain, seq)
    lam_rows = jnp.pad(jnp.stack([lam_q1, lam_k1, lam_q2, lam_k2], axis=0),
                       ((0, 4), (0, LANES - DIFF_QK_DIM)))
    o_b = _flash(q2, kh, vh, lam_rows, subln_gain, seq, lambda_init)

    n_experts = w_router.shape[1]
    wr = jnp.pad(w_router, ((0, 0), (0, LANES - n_experts)))
    br = jnp.pad(b_router, (0, LANES - n_experts)).reshape(1, LANES)
    x1, h2, top_idx, top_w = _merge(
        x2, o_a.reshape(t, RWKV_WIDTH), o_b.reshape(t, DIFF_WIDTH), proj,
        w_branch_a.astype(BF16), w_branch_b.astype(BF16), w_out.astype(BF16),
        g1, norm2_gain.reshape(1, d), sc2, sh2, wr, br, seq, n_experts)

    out = _moe(x1, h2, top_idx, top_w, g2, w_exp_gate, b_exp_gate, w_exp_up, b_exp_up,
               w_exp_down, b_exp_down, seq)
    return out.reshape(bsz, seq, d)


def kernel(x, c, w_ada, b_ada, norm1_gain, norm2_gain, w_in, shift_mu, w0, w_decay_up, a0, w_aaa_up, w_gate_up, k_k, k_a, r_k, gn_w, gn_b, q_gain, k_gain, lam_q1, lam_k1, lam_q2, lam_k2, subln_gain, w_branch_a, w_branch_b, w_out, w_router, b_router, w_exp_gate, b_exp_gate, w_exp_up, b_exp_up, w_exp_down, b_exp_down):
    per_layer = (w_ada, b_ada, norm1_gain, norm2_gain, w_in, shift_mu, w0, w_decay_up, a0,
                 w_aaa_up, w_gate_up, k_k, k_a, r_k, gn_w, gn_b, q_gain, k_gain, lam_q1, lam_k1,
                 lam_q2, lam_k2, subln_gain, w_branch_a, w_branch_b, w_out, w_router, b_router,
                 w_exp_gate, b_exp_gate, w_exp_up, b_exp_up, w_exp_down, b_exp_down)
    for l in range(w_in.shape[0]):
        x = _layer(x, c, l, *(p[l] for p in per_layer))
    return x
```

```python
import functools
import math

import jax
import jax.numpy as jnp
from jax import lax
from jax.experimental import pallas as pl
from jax.experimental.pallas import tpu as pltpu

F32 = jnp.float32
BF16 = jnp.bfloat16

NORM_EPS = 1e-6
CHUNK = 64

RWKV_HEADS = 16
RWKV_HEAD_DIM = 64
RWKV_WIDTH = RWKV_HEADS * RWKV_HEAD_DIM
RWKV_GN_EPS = 64e-5

DIFF_HEADS = 8
DIFF_QK_DIM = 64
DIFF_V_DIM = 128
DIFF_WIDTH = DIFF_HEADS * DIFF_V_DIM
ROPE_THETA = 500000.0
ROPE_DIM = DIFF_QK_DIM // 4

TOP_K = 4
SWIGLU_LIMIT = 7.0
SWIGLU_ALPHA = 1.702

LANES = 128
LORA_PACK = 512
VMEM_LIMIT = 56 * 1024 * 1024

OFF_R, OFF_K, OFF_V = 0, 1024, 2048
OFF_QD, OFF_KD, OFF_VD = 3072, 4096, 5120
OFF_GA, OFF_GB = 6144, 8192
OFF_LORA = 10240
PACKED_WIDTH = OFF_LORA + LORA_PACK


def _cparams(sem):
    return pltpu.CompilerParams(dimension_semantics=sem, vmem_limit_bytes=VMEM_LIMIT)


def _nt_dot(a, b):
    return lax.dot_general(a, b, (((1,), (1,)), ((), ())), preferred_element_type=F32)


def _dot(a, b):
    return jnp.dot(a, b, preferred_element_type=F32)


def _ada_kernel(c_ref, w_ref, b_ref, o_ref):
    c = c_ref[...]
    s = c * jax.nn.sigmoid(c)
    o_ref[...] = _dot(s.astype(BF16), w_ref[...].astype(BF16)) + b_ref[...]


def _ada(c, w, b):
    bsz, d = c.shape
    n = w.shape[1]
    rows = 8
    tn = 1024
    cp = jnp.pad(c, ((0, rows - bsz), (0, 0)))
    out = pl.pallas_call(
        _ada_kernel,
        grid=(n // tn,),
        in_specs=[pl.BlockSpec((rows, d), lambda j: (0, 0)),
                  pl.BlockSpec((d, tn), lambda j: (0, j)),
                  pl.BlockSpec((1, tn), lambda j: (0, j))],
        out_specs=pl.BlockSpec((rows, tn), lambda j: (0, j)),
        out_shape=jax.ShapeDtypeStruct((rows, n), F32),
        compiler_params=_cparams(("arbitrary",)),
        name="ada",
    )(cp, w, b.reshape(1, n))
    return out[:bsz]


def _inproj_kernel(x_ref, g_ref, sc_ref, sh_ref, w_ref, mu_ref, o_ref, h_scr, carry_scr,
                   *, tiles_per_batch, n_rkv_tiles, lora_tile):
    i = pl.program_id(0)
    j = pl.program_id(1)

    @pl.when(j == 0)
    def _():
        x = x_ref[...]
        ms = jnp.mean(x * x, axis=-1, keepdims=True)
        y = x * lax.rsqrt(ms + NORM_EPS) * g_ref[...]
        h_scr[...] = (y * (1.0 + sc_ref[0]) + sh_ref[0]).astype(BF16)

    p = _dot(h_scr[...], w_ref[...])
    shifted = jnp.logical_or(j < n_rkv_tiles, j == lora_tile)

    @pl.when(shifted)
    def _():
        slot = jnp.where(j == lora_tile, n_rkv_tiles, j)

        @pl.when(i % tiles_per_batch == 0)
        def _():
            carry_scr[slot] = jnp.zeros(carry_scr.shape[1:], F32)

        prev_last = carry_scr[slot]
        tm = p.shape[0]
        carry_scr[slot] = p[tm - 1:tm, :]
        row = lax.broadcasted_iota(jnp.int32, p.shape, 0)
        prev = jnp.where(row == 0, prev_last, pltpu.roll(p, 1, 0))
        o_ref[...] = (p + (prev - p) * mu_ref[...]).astype(o_ref.dtype)

    @pl.when(jnp.logical_not(shifted))
    def _():
        o_ref[...] = p.astype(o_ref.dtype)


def _inproj(x2, gain, sc, sh, w_packed, mu_packed, seq):
    t, d = x2.shape
    tm = min(1024, seq)
    tn = 512
    n_col = PACKED_WIDTH // tn
    n_rkv_tiles = OFF_QD // tn
    lora_tile = OFF_LORA // tn
    tiles_per_batch = seq // tm

    def mu_map(i, j):
        return (0, jnp.where(j == lora_tile, n_rkv_tiles, jnp.minimum(j, n_rkv_tiles - 1)))

    kern = functools.partial(_inproj_kernel, tiles_per_batch=tiles_per_batch,
                             n_rkv_tiles=n_rkv_tiles, lora_tile=lora_tile)
    return pl.pallas_call(
        kern,
        grid=(t // tm, n_col),
        in_specs=[pl.BlockSpec((tm, d), lambda i, j: (i, 0)),
                  pl.BlockSpec((1, d), lambda i, j: (0, 0)),
                  pl.BlockSpec((1, 1, d), lambda i, j: (i // tiles_per_batch, 0, 0)),
                  pl.BlockSpec((1, 1, d), lambda i, j: (i // tiles_per_batch, 0, 0)),
                  pl.BlockSpec((d, tn), lambda i, j: (0, j)),
                  pl.BlockSpec((1, tn), mu_map)],
        out_specs=pl.BlockSpec((tm, tn), lambda i, j: (i, j)),
        out_shape=jax.ShapeDtypeStruct((t, PACKED_WIDTH), BF16),
        scratch_shapes=[pltpu.VMEM((tm, d), BF16),
                        pltpu.VMEM((n_rkv_tiles + 1, 1, tn), F32)],
        compiler_params=_cparams(("arbitrary", "arbitrary")),
        name="inproj",
    )(x2, gain, sc, sh, w_packed, mu_packed)


def _pack_in_weights(w_in, shift_mu, decay_rank, aaa_rank, gate_rank):
    assert decay_rank <= LANES and aaa_rank <= LANES and gate_rank == 2 * LANES
    rw = RWKV_WIDTH
    o = 3 * rw
    w_r, w_k, w_v = w_in[:, 0:rw], w_in[:, rw:2 * rw], w_in[:, 2 * rw:3 * rw]
    w_dl = w_in[:, o:o + decay_rank]
    w_al = w_in[:, o + decay_rank:o + decay_rank + aaa_rank]
    w_gl = w_in[:, o + decay_rank + aaa_rank:o + decay_rank + aaa_rank + gate_rank]
    rest = w_in[:, o + decay_rank + aaa_rank + gate_rank:]

    def padc(a, width):
        return jnp.pad(a, ((0, 0), (0, width - a.shape[1])))

    packed = jnp.concatenate(
        [w_r, w_k, w_v, rest, padc(w_dl, LANES), padc(w_al, LANES), w_gl], axis=1).astype(BF16)
    assert packed.shape[1] == PACKED_WIDTH
    mu = shift_mu.reshape(1, -1)
    mu_packed = jnp.concatenate(
        [mu[:, :o], padc(mu[:, o:o + decay_rank], LANES),
         padc(mu[:, o + decay_rank:o + decay_rank + aaa_rank], LANES),
         mu[:, o + decay_rank + aaa_rank:]], axis=1)
    return packed, mu_packed


def _split3_bf16(x):
    hi = x.astype(BF16)
    r1 = x - hi.astype(F32)
    mid = r1.astype(BF16)
    lo = (r1 - mid.astype(F32)).astype(BF16)
    return hi, mid, lo


def _rwkv_kernel(r_ref, k_ref, v_ref, lora_ref, wd_ref, wa_ref, wg_ref, prm_ref, o_ref, st_ref,
                 *, tm):
    c = CHUNK
    n = RWKV_HEAD_DIM
    n_chunks = tm // c

    @pl.when(pl.program_id(2) == 0)
    def _():
        st_ref[...] = jnp.zeros(st_ref.shape, F32)

    lane = lax.broadcasted_iota(jnp.int32, (1, LANES), 1)
    m0 = lane < n
    m0_256 = jnp.concatenate([m0, m0], axis=1)

    def head_sum(x):
        s0 = jnp.sum(jnp.where(m0, x, 0.0), axis=-1, keepdims=True)
        s1 = jnp.sum(jnp.where(m0, 0.0, x), axis=-1, keepdims=True)
        return jnp.where(m0, s0, s1)

    prm = prm_ref[...]
    w0, a0, k_k, k_a, r_k, gn_w, gn_b = (prm[i:i + 1] for i in range(7))

    r = r_ref[0].astype(F32)
    k = k_ref[0].astype(F32)
    v = v_ref[0].astype(F32)
    lora = lora_ref[0]
    d_code = lora[:, 0:LANES].astype(F32)
    a_code = lora[:, LANES:2 * LANES]
    g_code = lora[:, 2 * LANES:4 * LANES].astype(F32)

    wl = w0 + _dot(jnp.tanh(d_code).astype(BF16), wd_ref[...])
    z = -wl
    softplus = jnp.maximum(z, 0.0) + jnp.log(1.0 + jnp.exp(-jnp.abs(z)))
    lw = -jnp.exp(-softplus - 0.5)
    a = jax.nn.sigmoid(a0 + _dot(a_code, wa_ref[...]))
    g = _dot(jax.nn.sigmoid(g_code).astype(BF16), wg_ref[...])

    kk = k * k_k
    kk = kk / jnp.maximum(jnp.sqrt(head_sum(kk * kk)), 1e-12)
    k2 = k * (1.0 + (a - 1.0) * k_a)
    a_vec = -kk
    b_vec = kk * a
    bonus = head_sum(r * k2 * r_k) * v

    tb = min(tm, 2 * LANES)
    ri = lax.broadcasted_iota(jnp.int32, (tb, tb), 0)
    ci = lax.broadcasted_iota(jnp.int32, (tb, tb), 1)
    tri = jnp.where(jnp.logical_and(ci <= ri, ci // c == ri // c), 1.0, 0.0).astype(BF16)
    lw3 = jnp.concatenate(_split3_bf16(lw), axis=1)
    cum3 = jnp.concatenate([_dot(tri, lw3[u * tb:(u + 1) * tb]) for u in range(tm // tb)], axis=0)
    cum = cum3[:, 0:LANES] + cum3[:, LANES:2 * LANES] + cum3[:, 2 * LANES:3 * LANES]
    cum_end = jnp.concatenate(
        [jnp.broadcast_to(cum[(q + 1) * c - 1:(q + 1) * c, :], (c, LANES)) for q in range(n_chunks)],
        axis=0)

    e_inc = jnp.exp(cum)
    e_exc = jnp.exp(cum - lw)
    e_neg = jnp.exp(-cum)
    e_end = jnp.exp(cum_end - cum)
    a_t = a_vec * e_exc
    r_t = r * e_inc
    k_h = k2 * e_neg
    b_h = b_vec * e_neg
    k_d = k2 * e_end
    b_d = b_vec * e_end
    d_end = jnp.exp(cum_end)

    row64 = lax.broadcasted_iota(jnp.int32, (c, LANES), 0)
    col64 = lax.broadcasted_iota(jnp.int32, (c, LANES), 1) % n
    strict = row64 > col64
    incl = row64 >= col64
    rb = lax.broadcasted_iota(jnp.int32, (LANES, LANES), 0) < n
    cb = lax.broadcasted_iota(jnp.int32, (LANES, LANES), 1) < n
    blockdiag = rb == cb

    def stack_heads(zz, mask):
        return jnp.concatenate([jnp.where(mask, zz, 0.0), jnp.where(mask, 0.0, zz)],
                               axis=0).astype(BF16)

    qs = range(n_chunks)
    ch = lambda t: [t[q * c:(q + 1) * c] for q in qs]
    at_c, rt_c, kh_c, bh_c, kd_c, bd_c, v_c = (ch(t) for t in (a_t, r_t, k_h, b_h, k_d, b_d, v))
    out0 = [_nt_dot(jnp.concatenate([jnp.where(m0, at_c[q], 0.0), jnp.where(m0, rt_c[q], 0.0)],
                                    axis=0).astype(BF16),
                    jnp.concatenate([bh_c[q], kh_c[q]], axis=0).astype(BF16)) for q in qs]
    out1 = [_nt_dot(jnp.concatenate([jnp.where(m0, 0.0, at_c[q]), jnp.where(m0, 0.0, rt_c[q])],
                                    axis=0).astype(BF16),
                    jnp.concatenate([kh_c[q], bh_c[q]], axis=0).astype(BF16)) for q in qs]
    x_cat = [jnp.where(strict, jnp.where(m0, out0[q][:c], out1[q][:c]), 0.0) for q in qs]
    ak_x = [jnp.where(strict, jnp.where(m0, out1[q][:c], out0[q][:c]), 0.0) for q in qs]
    rb_cat = [jnp.where(incl, jnp.where(m0, out0[q][c:], out1[q][c:]), 0.0) for q in qs]
    rk_x = [jnp.where(incl, jnp.where(m0, out1[q][c:], out0[q][c:]), 0.0) for q in qs]
    w12 = [_dot(jnp.concatenate([ak_x[q], rk_x[q]], axis=0).astype(BF16),
                jnp.concatenate([jnp.where(m0, 0.0, v_c[q]), jnp.where(m0, v_c[q], 0.0)],
                                axis=0).astype(BF16)) for q in qs]

    zz = [jnp.concatenate([at_c[q], w12[q][:c]], axis=1) for q in qs]
    xp = x_cat
    for step in range(6):
        zz = [zz[q] + _dot(xp[q].astype(BF16), stack_heads(zz[q], m0_256)) for q in qs]
        if step < 5:
            xp = [_dot(xp[q].astype(BF16), stack_heads(xp[q], m0)) for q in qs]
    rbz = [_dot(rb_cat[q].astype(BF16), stack_heads(zz[q], m0_256)) for q in qs]
    q_c = [rt_c[q] + rbz[q][:, 0:LANES] for q in qs]
    z_c = [w12[q][c:] + rbz[q][:, LANES:] for q in qs]
    mn = []
    for q in qs:
        t_src = jnp.concatenate(
            [zz[q], jnp.concatenate([jnp.zeros_like(v_c[q]), v_c[q]], axis=1)], axis=0)
        mn.append(_dot(t_src.T.astype(BF16),
                       jnp.concatenate([bd_c[q], kd_c[q]], axis=0).astype(BF16)))
    m_x = [jnp.where(blockdiag, mn[q][0:LANES], 0.0).astype(BF16) for q in qs]
    n_x = [jnp.where(blockdiag, mn[q][LANES:], 0.0) for q in qs]

    state = st_ref[...]
    ys = []
    for q in qs:
        s_bf = state.astype(BF16)
        ys.append(_nt_dot(q_c[q].astype(BF16), s_bf) + z_c[q])
        state = state * d_end[q * c:q * c + 1, :] + _dot(s_bf, m_x[q]) + n_x[q]

    st_ref[...] = state
    y = jnp.concatenate(ys, axis=0)
    mu = head_sum(y) * (1.0 / n)
    yc = y - mu
    var = head_sum(yc * yc) * (1.0 / n)
    yn = yc * lax.rsqrt(var + RWKV_GN_EPS) * gn_w + gn_b
    o_ref[0] = ((yn + bonus) * g).astype(o_ref.dtype)


def _rwkv(proj3, wd, wa, wg, prm, seq):
    bsz = proj3.shape[0]
    tm = min(1024, seq)
    n_pairs = RWKV_WIDTH // LANES
    blk = lambda off: (lambda b, p, t: (b, t, off // LANES + p))
    kern = functools.partial(_rwkv_kernel, tm=tm)
    return pl.pallas_call(
        kern,
        grid=(bsz, n_pairs, seq // tm),
        in_specs=[pl.BlockSpec((1, tm, LANES), blk(OFF_R)),
                  pl.BlockSpec((1, tm, LANES), blk(OFF_K)),
                  pl.BlockSpec((1, tm, LANES), blk(OFF_V)),
                  pl.BlockSpec((1, tm, LORA_PACK), lambda b, p, t: (b, t, OFF_LORA // LORA_PACK)),
                  pl.BlockSpec((LANES, LANES), lambda b, p, t: (0, p)),
                  pl.BlockSpec((LANES, LANES), lambda b, p, t: (0, p)),
                  pl.BlockSpec((2 * LANES, LANES), lambda b, p, t: (0, p)),
                  pl.BlockSpec((8, LANES), lambda b, p, t: (0, p))],
        out_specs=pl.BlockSpec((1, tm, LANES), lambda b, p, t: (b, t, p)),
        out_shape=jax.ShapeDtypeStruct((bsz, seq, RWKV_WIDTH), BF16),
        scratch_shapes=[pltpu.VMEM((LANES, LANES), F32)],
        compiler_params=_cparams(("arbitrary", "arbitrary", "arbitrary")),
        name="rwkv",
    )(proj3, proj3, proj3, proj3, wd, wa, wg, prm)


def _attn_prep_kernel(q_ref, k_ref, v_ref, cos_ref, sa_ref, sb_ref, gavg_ref, qg_ref, kg_ref,
                      q2_ref, ko_ref, vo_ref):
    width = q_ref.shape[-1]
    reps = width // LANES
    tile = lambda t: jnp.concatenate([t] * reps, axis=1)
    cos_t, sin_a, sin_b = tile(cos_ref[...]), tile(sa_ref[...]), tile(sb_ref[...])
    gavg = gavg_ref[...]
    half = ROPE_DIM // 2

    def norm_rope(x, gain):
        x = x.astype(F32)
        sq = x * x
        hi = sq.astype(BF16)
        lo = (sq - hi.astype(F32)).astype(BF16)
        ms = _dot(hi, gavg) + _dot(lo, gavg)
        y = x * lax.rsqrt(ms + NORM_EPS) * gain
        return (y * cos_t + pltpu.roll(y, width - half, 1) * sin_a + pltpu.roll(y, half, 1) * sin_b)

    qn = norm_rope(q_ref[0], qg_ref[...])
    kn = norm_rope(k_ref[0], kg_ref[...])
    lane = lax.broadcasted_iota(jnp.int32, (1, LANES), 1)
    m0 = lane < DIFF_QK_DIM
    for h in range(DIFF_HEADS):
        sl = slice(h * LANES, (h + 1) * LANES)
        qh = qn[:, sl]
        q2_ref[0, h, 0] = jnp.where(m0, qh, 0.0).astype(BF16)
        q2_ref[0, h, 1] = jnp.where(m0, 0.0, qh).astype(BF16)
        ko_ref[0, h] = kn[:, sl].astype(BF16)
        vo_ref[0, h] = v_ref[0, :, sl]


def _rope_tables(seq):
    half = ROPE_DIM // 2
    inv_freq = ROPE_THETA ** (-jnp.arange(half, dtype=F32) * 2.0 / ROPE_DIM)
    ang = jnp.arange(seq, dtype=jnp.int32).astype(F32)[:, None] * inv_freq[None, :]
    cos, sin = jnp.cos(ang), jnp.sin(ang)
    ones = jnp.ones((seq, DIFF_QK_DIM - ROPE_DIM), F32)
    zeros = jnp.zeros((seq, DIFF_QK_DIM - ROPE_DIM), F32)
    zh = jnp.zeros((seq, half), F32)
    cos64 = jnp.concatenate([cos, cos, ones], axis=1)
    sa64 = jnp.concatenate([-sin, zh, zeros], axis=1)
    sb64 = jnp.concatenate([zh, sin, zeros], axis=1)
    dup = lambda t: jnp.concatenate([t, t], axis=1)
    return dup(cos64), dup(sa64), dup(sb64)


def _attn_prep(proj3, q_gain, k_gain, seq):
    bsz = proj3.shape[0]
    tm = min(512, seq)
    width = DIFF_HEADS * 2 * DIFF_QK_DIM
    cos_t, sin_a, sin_b = _rope_tables(seq)
    gi = jnp.arange(width) // DIFF_QK_DIM
    gavg = jnp.where(gi[:, None] == gi[None, :], 1.0 / DIFF_QK_DIM, 0.0).astype(BF16)
    reps = width // DIFF_QK_DIM
    qg = (jnp.tile(q_gain, reps) * (DIFF_QK_DIM ** -0.5 * math.log2(math.e))).reshape(1, width)
    kg = jnp.tile(k_gain, reps).reshape(1, width)
    hd = (bsz, DIFF_HEADS, seq, LANES)
    tab = pl.BlockSpec((tm, LANES), lambda b, t: (t, 0))
    return pl.pallas_call(
        _attn_prep_kernel,
        grid=(bsz, seq // tm),
        in_specs=[pl.BlockSpec((1, tm, width), lambda b, t: (b, t, OFF_QD // width)),
                  pl.BlockSpec((1, tm, width), lambda b, t: (b, t, OFF_KD // width)),
                  pl.BlockSpec((1, tm, width), lambda b, t: (b, t, OFF_VD // width)),
                  tab, tab, tab,
                  pl.BlockSpec((width, width), lambda b, t: (0, 0)),
                  pl.BlockSpec((1, width), lambda b, t: (0, 0)),
                  pl.BlockSpec((1, width), lambda b, t: (0, 0))],
        out_specs=[pl.BlockSpec((1, DIFF_HEADS, 2, tm, LANES), lambda b, t: (b, 0, 0, t, 0)),
                   pl.BlockSpec((1, DIFF_HEADS, tm, LANES), lambda b, t: (b, 0, t, 0)),
                   pl.BlockSpec((1, DIFF_HEADS, tm, LANES), lambda b, t: (b, 0, t, 0))],
        out_shape=[jax.ShapeDtypeStruct((bsz, DIFF_HEADS, 2, seq, LANES), BF16),
                   jax.ShapeDtypeStruct(hd, BF16), jax.ShapeDtypeStruct(hd, BF16)],
        compiler_params=_cparams(("arbitrary", "arbitrary")),
        name="attn_prep",
    )(proj3, proj3, proj3, cos_t, sin_a, sin_b, gavg, qg, kg)


def _flash_kernel(qi_ref, kj_ref, q_ref, k_ref, v_ref, lam_ref, sg_ref, o_ref, m_ref, acc_ref,
                  *, tq, tk, lambda_init):
    pidx = pl.program_id(2)
    i = qi_ref[pidx]
    j = kj_ref[pidx]

    @pl.when(j == 0)
    def _():
        m_ref[...] = jnp.full(m_ref.shape, -jnp.inf, F32)
        acc_ref[...] = jnp.zeros(acc_ref.shape, F32)

    q = q_ref[0, 0].reshape(2 * tq, LANES)
    s = _nt_dot(q, k_ref[0, 0]).astype(BF16)

    def update(sc):
        m_old = m_ref[...]
        m_new = jnp.maximum(m_old, jnp.max(sc, axis=-1, keepdims=True).astype(F32))
        alpha = jnp.exp2(m_old - m_new)
        m_b = m_new.astype(BF16)
        p = jnp.exp2(sc - jnp.concatenate([m_b] * (tk // LANES), axis=1))
        v = v_ref[0, 0]
        v_ext = jnp.concatenate([v, jnp.ones_like(v)], axis=1)
        acc_ref[...] = jnp.concatenate([alpha, alpha], axis=1) * acc_ref[...] + _dot(p, v_ext)
        m_ref[...] = m_new

    @pl.when(j < i)
    def _():
        update(s)

    @pl.when(j == i)
    def _():
        qpos = lax.broadcasted_iota(jnp.int32, s.shape, 0) % tq
        kpos = lax.broadcasted_iota(jnp.int32, s.shape, 1)
        update(jnp.where(kpos // CHUNK <= qpos // CHUNK, s, -jnp.inf).astype(BF16))
        lam_rows = lam_ref[...]
        lam = (jnp.exp(jnp.sum(lam_rows[0:1] * lam_rows[1:2], axis=-1, keepdims=True))
               - jnp.exp(jnp.sum(lam_rows[2:3] * lam_rows[3:4], axis=-1, keepdims=True))
               + lambda_init)
        acc = acc_ref[...]
        o1 = acc[0:tq, 0:LANES] / acc[0:tq, LANES:]
        o2 = acc[tq:, 0:LANES] / acc[tq:, LANES:]
        o = o1 - lam * o2
        ms = jnp.mean(o * o, axis=-1, keepdims=True)
        o_ref[0] = (o * lax.rsqrt(ms + NORM_EPS) * sg_ref[...] * (1.0 - lambda_init)).astype(o_ref.dtype)


def _flash(q2, kh, vh, lam_rows, subln_gain, seq, lambda_init):
    bsz = q2.shape[0]
    tq = tk = min(1024, seq)
    nq = seq // tq
    pairs = [(i, j) for i in range(nq) for j in range(i + 1)]
    qi = jnp.asarray([p[0] for p in pairs], jnp.int32)
    kj = jnp.asarray([p[1] for p in pairs], jnp.int32)
    kern = functools.partial(_flash_kernel, tq=tq, tk=tk, lambda_init=lambda_init)
    grid_spec = pltpu.PrefetchScalarGridSpec(
        num_scalar_prefetch=2,
        grid=(bsz, DIFF_HEADS, len(pairs)),
        in_specs=[pl.BlockSpec((1, 1, 2, tq, LANES), lambda b, h, p, qi, kj: (b, h, 0, qi[p], 0)),
                  pl.BlockSpec((1, 1, tk, LANES), lambda b, h, p, qi, kj: (b, h, kj[p], 0)),
                  pl.BlockSpec((1, 1, tk, LANES), lambda b, h, p, qi, kj: (b, h, kj[p], 0)),
                  pl.BlockSpec((8, LANES), lambda b, h, p, qi, kj: (0, 0)),
                  pl.BlockSpec((1, LANES), lambda b, h, p, qi, kj: (0, 0))],
        out_specs=pl.BlockSpec((1, tq, LANES), lambda b, h, p, qi, kj: (b, qi[p], h)),
        scratch_shapes=[pltpu.VMEM((2 * tq, LANES), F32), pltpu.VMEM((2 * tq, 2 * LANES), F32)],
    )
    return pl.pallas_call(
        kern,
        grid_spec=grid_spec,
        out_shape=jax.ShapeDtypeStruct((bsz, seq, DIFF_WIDTH), BF16),
        compiler_params=_cparams(("arbitrary", "arbitrary", "arbitrary")),
        name="flash",
    )(qi, kj, q2, kh, vh, lam_rows, subln_gain.reshape(1, LANES))


def _merge_kernel(x_ref, oa_ref, ob_ref, ga_ref, gb_ref, wa_ref, wb_ref, wo_ref, g1_ref,
                  n2_ref, sc2_ref, sh2_ref, wr2_ref, wrh_ref, br_ref,
                  x1_ref, h2_ref, idx_ref, tw_ref, *, n_experts):
    ma = _dot(oa_ref[...], wa_ref[...])
    mb = _dot(ob_ref[...], wb_ref[...])
    merged = (jax.nn.sigmoid(ga_ref[...].astype(F32)) * ma
              + jax.nn.sigmoid(gb_ref[...].astype(F32)) * mb)
    x1 = x_ref[...] + g1_ref[0] * _dot(merged.astype(BF16), wo_ref[...])
    x1_ref[...] = x1
    ms = jnp.mean(x1 * x1, axis=-1, keepdims=True)
    h2 = x1 * lax.rsqrt(ms + NORM_EPS) * n2_ref[...] * (1.0 + sc2_ref[0]) + sh2_ref[0]
    h2_ref[...] = h2

    h_hi = h2.astype(BF16)
    h_lo = (h2 - h_hi.astype(F32)).astype(BF16)
    hh = _dot(h_hi, wr2_ref[...])
    logits = hh[:, 0:LANES] + hh[:, LANES:] + _dot(h_lo, wrh_ref[...]) + br_ref[...]
    lane = lax.broadcasted_iota(jnp.int32, logits.shape, 1)
    lane_f = lane.astype(F32)
    lg = jnp.where(lane < n_experts, logits, -jnp.inf)
    vals, idxs = [], []
    for _ in range(TOP_K):
        mx = jnp.max(lg, axis=-1, keepdims=True)
        ix = jnp.min(jnp.where(lg == mx, lane_f, float(LANES)), axis=-1, keepdims=True)
        vals.append(mx)
        idxs.append(ix)
        lg = jnp.where(lane_f == ix, -jnp.inf, lg)
    exps = [jnp.exp(vv - vals[0]) for vv in vals]
    denom = exps[0] + exps[1] + exps[2] + exps[3]
    idx_out = jnp.zeros(logits.shape, F32)
    tw_out = jnp.zeros(logits.shape, F32)
    for kk in range(TOP_K):
        idx_out = jnp.where(lane == kk, idxs[kk], idx_out)
        tw_out = jnp.where(lane == kk, exps[kk] / denom, tw_out)
    idx_ref[...] = idx_out.astype(jnp.int32)
    tw_ref[...] = tw_out


def _merge(x2, oa, ob, proj, wa, wb, wo, g1, n2, sc2, sh2, wr, br, seq, n_experts):
    t, d = x2.shape
    tm = min(256, seq)
    tpb = seq // tm
    gw = d
    wr_hi = wr.astype(BF16)
    wr_lo = (wr - wr_hi.astype(F32)).astype(BF16)
    wr2 = jnp.concatenate([wr_hi, wr_lo], axis=1)
    const = lambda shape: pl.BlockSpec(shape, lambda i: tuple(0 for _ in shape),
                                       pipeline_mode=pl.Buffered(1))
    bvec = pl.BlockSpec((1, 1, d), lambda i: (i // tpb, 0, 0))
    kern = functools.partial(_merge_kernel, n_experts=n_experts)
    return pl.pallas_call(
        kern,
        grid=(t // tm,),
        in_specs=[pl.BlockSpec((tm, d), lambda i: (i, 0)),
                  pl.BlockSpec((tm, RWKV_WIDTH), lambda i: (i, 0)),
                  pl.BlockSpec((tm, DIFF_WIDTH), lambda i: (i, 0)),
                  pl.BlockSpec((tm, gw), lambda i: (i, OFF_GA // gw)),
                  pl.BlockSpec((tm, gw), lambda i: (i, OFF_GB // gw)),
                  const(wa.shape), const(wb.shape), const(wo.shape),
                  bvec, const((1, d)), bvec, bvec,
                  const(wr2.shape), const(wr_hi.shape), const(br.shape)],
        out_specs=[pl.BlockSpec((tm, d), lambda i: (i, 0)),
                   pl.BlockSpec((tm, d), lambda i: (i, 0)),
                   pl.BlockSpec((tm, LANES), lambda i: (i, 0)),
                   pl.BlockSpec((tm, LANES), lambda i: (i, 0))],
        out_shape=[jax.ShapeDtypeStruct((t, d), F32), jax.ShapeDtypeStruct((t, d), F32),
                   jax.ShapeDtypeStruct((t, LANES), jnp.int32),
                   jax.ShapeDtypeStruct((t, LANES), F32)],
        compiler_params=_cparams(("arbitrary",)),
        name="merge",
    )(x2, oa, ob, proj, proj, wa, wb, wo, g1, n2, sc2, sh2, wr2, wr_hi, br)


def _row_copy(src_hbm, dst_ref, src_row, dst_row, sem):
    return pltpu.make_async_copy(src_hbm.at[pl.ds(src_row, 1)], dst_ref.at[pl.ds(dst_row, 1)], sem)


def _expert_kernel(be_ref, na_ref, tok0_ref, tokn_ref, h_hbm, wg_ref, bg_ref, wu_ref, bu_ref,
                   wd_ref, bd_ref, ys_ref, xs_scr, xb_scr, acc_scr, sems, *, bm, rows_per_step):
    i = pl.program_id(0)
    f = pl.program_id(1)
    n_act = na_ref[0]
    slot = i % 2

    def wait_slot(s):
        pltpu.make_async_copy(xs_scr.at[s], xs_scr.at[s], sems.at[s]).wait()

    @pl.when(i < n_act)
    def _():
        @pl.when(f == 0)
        def _():
            @pl.when(i == 0)
            def _():
                def issue(r, carry):
                    _row_copy(h_hbm, xs_scr.at[0], tok0_ref[0, 0, r], r, sems.at[0]).start()
                    return carry

                lax.fori_loop(0, bm, issue, 0)

            wait_slot(slot)
            xb_scr[...] = xs_scr[slot].astype(BF16)
            acc_scr[...] = jnp.zeros(acc_scr.shape, F32)

        for r in range(rows_per_step):
            _row_copy(h_hbm, xs_scr.at[1 - slot], tokn_ref[0, 0, r], f * rows_per_step + r,
                      sems.at[1 - slot]).start()

        xb = xb_scr[...]
        gt = jnp.minimum(_dot(xb, wg_ref[0]) + bg_ref[0], SWIGLU_LIMIT)
        up = jnp.clip(_dot(xb, wu_ref[0]) + bu_ref[0], -SWIGLU_LIMIT, SWIGLU_LIMIT)
        act = (up + 1.0) * gt * jax.nn.sigmoid(SWIGLU_ALPHA * gt)
        acc_scr[...] += _dot(act.astype(BF16), wd_ref[0])

        @pl.when(f == pl.num_programs(1) - 1)
        def _():
            ys_ref[...] = acc_scr[...] + bd_ref[0]

            @pl.when(i == n_act - 1)
            def _():
                wait_slot(1 - slot)

    @pl.when(jnp.logical_and(i >= n_act, f == 0))
    def _():
        ys_ref[...] = jnp.zeros(ys_ref.shape, F32)


def _experts(h2, row_token, block_expert, n_active, wg, bg, wu, bu, wd, bd, bm):
    d = h2.shape[1]
    n_exp, _, ff = wg.shape
    tf = 512
    nf = ff // tf
    rows_per_step = bm // nf
    p = row_token.shape[0]
    n_blocks = p // bm

    def row(i, na):
        return jnp.minimum(i, na[0] - 1)

    def next_rows(i, f, be, na):
        return (jnp.minimum(i + 1, n_blocks - 1) * nf + f, 0, 0)

    kern = functools.partial(_expert_kernel, bm=bm, rows_per_step=rows_per_step)
    grid_spec = pltpu.PrefetchScalarGridSpec(
        num_scalar_prefetch=2,
        grid=(n_blocks, nf),
        in_specs=[pl.BlockSpec((1, 1, bm), lambda i, f, be, na: (0, 0, 0), memory_space=pltpu.SMEM),
                  pl.BlockSpec((1, 1, rows_per_step), next_rows, memory_space=pltpu.SMEM),
                  pl.BlockSpec(memory_space=pl.ANY),
                  pl.BlockSpec((1, d, tf), lambda i, f, be, na: (be[row(i, na)], 0, f)),
                  pl.BlockSpec((1, 1, tf), lambda i, f, be, na: (be[row(i, na)], 0, f)),
                  pl.BlockSpec((1, d, tf), lambda i, f, be, na: (be[row(i, na)], 0, f)),
                  pl.BlockSpec((1, 1, tf), lambda i, f, be, na: (be[row(i, na)], 0, f)),
                  pl.BlockSpec((1, tf, d), lambda i, f, be, na: (be[row(i, na)], f, 0)),
                  pl.BlockSpec((1, 1, d), lambda i, f, be, na: (be[row(i, na)], 0, 0))],
        out_specs=pl.BlockSpec((bm, d), lambda i, f, be, na: (i, 0)),
        scratch_shapes=[pltpu.VMEM((2, bm, d), F32), pltpu.VMEM((bm, d), BF16),
                        pltpu.VMEM((bm, d), F32), pltpu.SemaphoreType.DMA((2,))],
    )
    return pl.pallas_call(
        kern,
        grid_spec=grid_spec,
        out_shape=jax.ShapeDtypeStruct((p, d), F32),
        compiler_params=_cparams(("arbitrary", "arbitrary")),
        name="experts",
    )(block_expert, n_active, row_token.reshape(n_blocks, 1, bm),
      row_token.reshape(n_blocks * nf, 1, rows_per_step), h2,
      wg, bg.reshape(n_exp, 1, ff), wu, bu.reshape(n_exp, 1, ff), wd, bd.reshape(n_exp, 1, d))


def _combine_kernel(pos_ref, x1_ref, tw_ref, g2_ref, ys_hbm, o_ref, rows_scr, sem, *, tc):
    def issue(r, carry):
        for kk in range(TOP_K):
            _row_copy(ys_hbm, rows_scr.at[kk], pos_ref[0, 0, r * TOP_K + kk], r, sem).start(
                priority=kk % 2)
        return carry

    lax.fori_loop(0, tc, issue, 0)
    pltpu.make_async_copy(rows_scr, rows_scr, sem).wait()
    tw = tw_ref[...]
    moe = tw[:, 0:1] * rows_scr[0]
    for kk in range(1, TOP_K):
        moe = moe + tw[:, kk:kk + 1] * rows_scr[kk]
    o_ref[...] = x1_ref[...] + g2_ref[0] * moe


def _combine(x1, top_w, g2, ys, pos, seq):
    t, d = x1.shape
    tc = min(128, seq)
    tpb = seq // tc
    n_steps = t // tc
    kern = functools.partial(_combine_kernel, tc=tc)
    return pl.pallas_call(
        kern,
        grid=(n_steps,),
        in_specs=[pl.BlockSpec((1, 1, tc * TOP_K), lambda i: (i, 0, 0), memory_space=pltpu.SMEM),
                  pl.BlockSpec((tc, d), lambda i: (i, 0)),
                  pl.BlockSpec((tc, LANES), lambda i: (i, 0)),
                  pl.BlockSpec((1, 1, d), lambda i: (i // tpb, 0, 0)),
                  pl.BlockSpec(memory_space=pl.ANY)],
        out_specs=pl.BlockSpec((tc, d), lambda i: (i, 0)),
        out_shape=jax.ShapeDtypeStruct((t, d), F32),
        scratch_shapes=[pltpu.VMEM((TOP_K, tc, d), F32), pltpu.SemaphoreType.DMA(())],
        compiler_params=_cparams(("arbitrary",)),
        name="combine",
    )(pos.reshape(n_steps, 1, tc * TOP_K), x1, top_w, g2, ys)


def _route(top_idx, n_experts, bm):
    t = top_idx.shape[0]
    m = t * TOP_K
    n_blocks = (m + n_experts * (bm - 1) + bm - 1) // bm
    flat_e = top_idx.reshape(m)
    onehot = (flat_e[:, None] == jnp.arange(n_experts, dtype=jnp.int32)[None, :]).astype(jnp.int32)
    csum = jnp.cumsum(onehot, axis=0)
    rank = jnp.take_along_axis(csum, flat_e[:, None], axis=1)[:, 0] - 1
    counts = csum[-1]
    padded = (counts + bm - 1) // bm * bm
    pad_end = jnp.cumsum(padded)
    pad_start = pad_end - padded
    dest = (pad_start[flat_e] + rank).astype(jnp.int32)
    row_token = jnp.zeros((n_blocks * bm,), jnp.int32).at[dest].set(
        jnp.arange(m, dtype=jnp.int32) // TOP_K)
    block_start = jnp.arange(n_blocks, dtype=jnp.int32) * bm
    block_expert = jnp.minimum(jnp.searchsorted(pad_end, block_start, side='right'),
                               n_experts - 1).astype(jnp.int32)
    n_active = (pad_end[-1] // bm).astype(jnp.int32).reshape(1)
    return row_token, dest, block_expert, n_active


def _moe(x1, h2, top_idx, top_w, g2, wg, bg, wu, bu, wd, bd, seq):
    n_experts = wg.shape[0]
    bm = 512
    row_token, dest, block_expert, n_active = _route(top_idx[:, :TOP_K], n_experts, bm)
    ys = _experts(h2, row_token, block_expert, n_active, wg.astype(BF16), bg, wu.astype(BF16), bu,
                  wd.astype(BF16), bd, bm)
    return _combine(x1, top_w, g2, ys, dest, seq)


def _layer(x, c, l, w_ada, b_ada, norm1_gain, norm2_gain, w_in, shift_mu, w0, w_decay_up, a0,
           w_aaa_up, w_gate_up, k_k, k_a, r_k, gn_w, gn_b, q_gain, k_gain, lam_q1, lam_k1,
           lam_q2, lam_k2, subln_gain, w_branch_a, w_branch_b, w_out, w_router, b_router,
           w_exp_gate, b_exp_gate, w_exp_up, b_exp_up, w_exp_down, b_exp_down):
    bsz, seq, d = x.shape
    t = bsz * seq
    mod = _ada(c, w_ada, b_ada)
    sh1, sc1, g1, sh2, sc2, g2 = (m.reshape(bsz, 1, d) for m in jnp.split(mod, 6, axis=-1))

    decay_rank, aaa_rank, gate_rank = w_decay_up.shape[0], w_aaa_up.shape[0], w_gate_up.shape[0]
    w_packed, mu_packed = _pack_in_weights(w_in, shift_mu, decay_rank, aaa_rank, gate_rank)
    x2 = x.reshape(t, d)
    proj = _inproj(x2, norm1_gain.reshape(1, d), sc1, sh1, w_packed, mu_packed, seq)
    proj3 = proj.reshape(bsz, seq, PACKED_WIDTH)

    padr = lambda w: jnp.pad(w, ((0, LANES - w.shape[0]), (0, 0))).astype(BF16)
    prm = jnp.stack([w0, a0, k_k, k_a, r_k, gn_w, gn_b, jnp.zeros_like(w0)], axis=0)
    o_a = _rwkv(proj3, padr(w_decay_up), padr(w_aaa_up), w_gate_up.astype(BF16), prm, seq)

    lambda_init = 0.8 - 0.6 * math.exp(-0.3 * l)
    q2, kh, vh = _attn_prep(proj3, q_gain, k_gain, seq)
    lam_rows = jnp.pad(jnp.stack([lam_q1, lam_k1, lam_q2, lam_k2], axis=0),
                       ((0, 4), (0, LANES - DIFF_QK_DIM)))
    o_b = _flash(q2, kh, vh, lam_rows, subln_gain, seq, lambda_init)

    n_experts = w_router.shape[1]
    wr = jnp.pad(w_router, ((0, 0), (0, LANES - n_experts)))
    br = jnp.pad(b_router, (0, LANES - n_experts)).reshape(1, LANES)
    x1, h2, top_idx, top_w = _merge(
        x2, o_a.reshape(t, RWKV_WIDTH), o_b.reshape(t, DIFF_WIDTH), proj,
        w_branch_a.astype(BF16), w_branch_b.astype(BF16), w_out.astype(BF16),
        g1, norm2_gain.reshape(1, d), sc2, sh2, wr, br, seq, n_experts)

    out = _moe(x1, h2, top_idx, top_w, g2, w_exp_gate, b_exp_gate, w_exp_up, b_exp_up,
               w_exp_down, b_exp_down, seq)
    return out.reshape(bsz, seq, d)


def kernel(x, c, w_ada, b_ada, norm1_gain, norm2_gain, w_in, shift_mu, w0, w_decay_up, a0, w_aaa_up, w_gate_up, k_k, k_a, r_k, gn_w, gn_b, q_gain, k_gain, lam_q1, lam_k1, lam_q2, lam_k2, subln_gain, w_branch_a, w_branch_b, w_out, w_router, b_router, w_exp_gate, b_exp_gate, w_exp_up, b_exp_up, w_exp_down, b_exp_down):
    per_layer = (w_ada, b_ada, norm1_gain, norm2_gain, w_in, shift_mu, w0, w_decay_up, a0,
                 w_aaa_up, w_gate_up, k_k, k_a, r_k, gn_w, gn_b, q_gain, k_gain, lam_q1, lam_k1,
                 lam_q2, lam_k2, subln_gain, w_branch_a, w_branch_b, w_out, w_router, b_router,
                 w_exp_gate, b_exp_gate, w_exp_up, b_exp_up, w_exp_down, b_exp_down)
    for l in range(w_in.shape[0]):
        x = _layer(x, c, l, *(p[l] for p in per_layer))
    return x
```

```python
import functools
import math

import jax
import jax.numpy as jnp
from jax import lax
from jax.experimental import pallas as pl
from jax.experimental.pallas import tpu as pltpu

F32 = jnp.float32
BF16 = jnp.bfloat16

NORM_EPS = 1e-6
CHUNK = 64

RWKV_HEADS = 16
RWKV_HEAD_DIM = 64
RWKV_WIDTH = RWKV_HEADS * RWKV_HEAD_DIM
RWKV_GN_EPS = 64e-5

DIFF_HEADS = 8
DIFF_QK_DIM = 64
DIFF_V_DIM = 128
DIFF_WIDTH = DIFF_HEADS * DIFF_V_DIM
ROPE_THETA = 500000.0
ROPE_DIM = DIFF_QK_DIM // 4

TOP_K = 4
SWIGLU_LIMIT = 7.0
SWIGLU_ALPHA = 1.702

LANES = 128
LORA_PACK = 512
VMEM_LIMIT = 56 * 1024 * 1024

OFF_R, OFF_K, OFF_V = 0, 1024, 2048
OFF_QD, OFF_KD, OFF_VD = 3072, 4096, 5120
OFF_GA, OFF_GB = 6144, 8192
OFF_LORA = 10240
PACKED_WIDTH = OFF_LORA + LORA_PACK


def _cparams(sem):
    return pltpu.CompilerParams(dimension_semantics=sem, vmem_limit_bytes=VMEM_LIMIT)


def _nt_dot(a, b):
    return lax.dot_general(a, b, (((1,), (1,)), ((), ())), preferred_element_type=F32)


def _dot(a, b):
    return jnp.dot(a, b, preferred_element_type=F32)


def _ada_kernel(c_ref, w_ref, b_ref, o_ref):
    c = c_ref[...]
    s = c * jax.nn.sigmoid(c)
    o_ref[...] = _dot(s.astype(BF16), w_ref[...].astype(BF16)) + b_ref[...]


def _ada(c, w, b):
    bsz, d = c.shape
    n = w.shape[1]
    rows = 8
    tn = 1024
    cp = jnp.pad(c, ((0, rows - bsz), (0, 0)))
    out = pl.pallas_call(
        _ada_kernel,
        grid=(n // tn,),
        in_specs=[pl.BlockSpec((rows, d), lambda j: (0, 0)),
                  pl.BlockSpec((d, tn), lambda j: (0, j)),
                  pl.BlockSpec((1, tn), lambda j: (0, j))],
        out_specs=pl.BlockSpec((rows, tn), lambda j: (0, j)),
        out_shape=jax.ShapeDtypeStruct((rows, n), F32),
        compiler_params=_cparams(("arbitrary",)),
        name="ada",
    )(cp, w, b.reshape(1, n))
    return out[:bsz]


def _inproj_kernel(x_ref, g_ref, sc_ref, sh_ref, w_ref, mu_ref, o_ref, h_scr, carry_scr,
                   *, tiles_per_batch, n_rkv_tiles, lora_tile):
    i = pl.program_id(0)
    j = pl.program_id(1)

    @pl.when(j == 0)
    def _():
        x = x_ref[...]
        ms = jnp.mean(x * x, axis=-1, keepdims=True)
        y = x * lax.rsqrt(ms + NORM_EPS) * g_ref[...]
        h_scr[...] = (y * (1.0 + sc_ref[0]) + sh_ref[0]).astype(BF16)

    p = _dot(h_scr[...], w_ref[...])
    shifted = jnp.logical_or(j < n_rkv_tiles, j == lora_tile)

    @pl.when(shifted)
    def _():
        slot = jnp.where(j == lora_tile, n_rkv_tiles, j)

        @pl.when(i % tiles_per_batch == 0)
        def _():
            carry_scr[slot] = jnp.zeros(carry_scr.shape[1:], F32)

        prev_last = carry_scr[slot]
        tm = p.shape[0]
        carry_scr[slot] = p[tm - 1:tm, :]
        row = lax.broadcasted_iota(jnp.int32, p.shape, 0)
        prev = jnp.where(row == 0, prev_last, pltpu.roll(p, 1, 0))
        o_ref[...] = (p + (prev - p) * mu_ref[...]).astype(o_ref.dtype)

    @pl.when(jnp.logical_not(shifted))
    def _():
        o_ref[...] = p.astype(o_ref.dtype)


def _inproj(x2, gain, sc, sh, w_packed, mu_packed, seq):
    t, d = x2.shape
    tm = min(1024, seq)
    tn = 1536
    assert PACKED_WIDTH % tn == 0 and OFF_QD % tn == 0 and PACKED_WIDTH - tn <= OFF_LORA
    n_col = PACKED_WIDTH // tn
    n_rkv_tiles = OFF_QD // tn
    lora_tile = n_col - 1
    tiles_per_batch = seq // tm

    kern = functools.partial(_inproj_kernel, tiles_per_batch=tiles_per_batch,
                             n_rkv_tiles=n_rkv_tiles, lora_tile=lora_tile)
    return pl.pallas_call(
        kern,
        grid=(t // tm, n_col),
        in_specs=[pl.BlockSpec((tm, d), lambda i, j: (i, 0)),
                  pl.BlockSpec((1, d), lambda i, j: (0, 0)),
                  pl.BlockSpec((1, 1, d), lambda i, j: (i // tiles_per_batch, 0, 0)),
                  pl.BlockSpec((1, 1, d), lambda i, j: (i // tiles_per_batch, 0, 0)),
                  pl.BlockSpec((d, tn), lambda i, j: (0, j)),
                  pl.BlockSpec((1, tn), lambda i, j: (0, j))],
        out_specs=pl.BlockSpec((tm, tn), lambda i, j: (i, j)),
        out_shape=jax.ShapeDtypeStruct((t, PACKED_WIDTH), BF16),
        scratch_shapes=[pltpu.VMEM((tm, d), BF16),
                        pltpu.VMEM((n_rkv_tiles + 1, 1, tn), F32)],
        compiler_params=_cparams(("arbitrary", "arbitrary")),
        name="inproj",
    )(x2, gain, sc, sh, w_packed, mu_packed)


def _pack_in_weights(w_in, shift_mu, decay_rank, aaa_rank, gate_rank):
    assert decay_rank <= LANES and aaa_rank <= LANES and gate_rank == 2 * LANES
    rw = RWKV_WIDTH
    o = 3 * rw
    w_r, w_k, w_v = w_in[:, 0:rw], w_in[:, rw:2 * rw], w_in[:, 2 * rw:3 * rw]
    w_dl = w_in[:, o:o + decay_rank]
    w_al = w_in[:, o + decay_rank:o + decay_rank + aaa_rank]
    w_gl = w_in[:, o + decay_rank + aaa_rank:o + decay_rank + aaa_rank + gate_rank]
    rest = w_in[:, o + decay_rank + aaa_rank + gate_rank:]

    def padc(a, width):
        return jnp.pad(a, ((0, 0), (0, width - a.shape[1])))

    packed = jnp.concatenate(
        [w_r, w_k, w_v, rest, padc(w_dl, LANES), padc(w_al, LANES), w_gl], axis=1).astype(BF16)
    assert packed.shape[1] == PACKED_WIDTH
    mu = shift_mu.reshape(1, -1)
    mu_packed = jnp.concatenate(
        [mu[:, :o], jnp.zeros((1, OFF_LORA - o), F32), padc(mu[:, o:o + decay_rank], LANES),
         padc(mu[:, o + decay_rank:o + decay_rank + aaa_rank], LANES),
         mu[:, o + decay_rank + aaa_rank:]], axis=1)
    return packed, mu_packed


def _split3_bf16(x):
    hi = x.astype(BF16)
    r1 = x - hi.astype(F32)
    mid = r1.astype(BF16)
    lo = (r1 - mid.astype(F32)).astype(BF16)
    return hi, mid, lo


def _rwkv_kernel(r_ref, k_ref, v_ref, lora_ref, wd_ref, wa_ref, wg_ref, prm_ref, o_ref, st_ref,
                 *, tm):
    c = CHUNK
    n = RWKV_HEAD_DIM
    n_chunks = tm // c

    @pl.when(pl.program_id(2) == 0)
    def _():
        st_ref[...] = jnp.zeros(st_ref.shape, F32)

    lane = lax.broadcasted_iota(jnp.int32, (1, LANES), 1)
    m0 = lane < n
    m0_256 = jnp.concatenate([m0, m0], axis=1)

    def head_sum(x):
        s0 = jnp.sum(jnp.where(m0, x, 0.0), axis=-1, keepdims=True)
        s1 = jnp.sum(jnp.where(m0, 0.0, x), axis=-1, keepdims=True)
        return jnp.where(m0, s0, s1)

    prm = prm_ref[...]
    w0, a0, k_k, k_a, r_k, gn_w, gn_b = (prm[i:i + 1] for i in range(7))

    r = r_ref[0].astype(F32)
    k = k_ref[0].astype(F32)
    v = v_ref[0].astype(F32)
    lora = lora_ref[0]
    d_code = lora[:, 0:LANES].astype(F32)
    a_code = lora[:, LANES:2 * LANES]
    g_code = lora[:, 2 * LANES:4 * LANES].astype(F32)

    wl = w0 + _dot(jnp.tanh(d_code).astype(BF16), wd_ref[...])
    z = -wl
    softplus = jnp.maximum(z, 0.0) + jnp.log(1.0 + jnp.exp(-jnp.abs(z)))
    lw = -jnp.exp(-softplus - 0.5)
    a = jax.nn.sigmoid(a0 + _dot(a_code, wa_ref[...]))
    g = _dot(jax.nn.sigmoid(g_code).astype(BF16), wg_ref[...])

    kk = k * k_k
    kk = kk / jnp.maximum(jnp.sqrt(head_sum(kk * kk)), 1e-12)
    k2 = k * (1.0 + (a - 1.0) * k_a)
    a_vec = -kk
    b_vec = kk * a
    bonus = head_sum(r * k2 * r_k) * v

    tb = min(tm, 2 * LANES)
    ri = lax.broadcasted_iota(jnp.int32, (tb, tb), 0)
    ci = lax.broadcasted_iota(jnp.int32, (tb, tb), 1)
    tri = jnp.where(jnp.logical_and(ci <= ri, ci // c == ri // c), 1.0, 0.0).astype(BF16)
    lw3 = jnp.concatenate(_split3_bf16(lw), axis=1)
    cum3 = jnp.concatenate([_dot(tri, lw3[u * tb:(u + 1) * tb]) for u in range(tm // tb)], axis=0)
    cum = cum3[:, 0:LANES] + cum3[:, LANES:2 * LANES] + cum3[:, 2 * LANES:3 * LANES]
    cum_end = jnp.concatenate(
        [jnp.broadcast_to(cum[(q + 1) * c - 1:(q + 1) * c, :], (c, LANES)) for q in range(n_chunks)],
        axis=0)

    e_inc = jnp.exp(cum)
    e_exc = jnp.exp(cum - lw)
    e_neg = jnp.exp(-cum)
    e_end = jnp.exp(cum_end - cum)
    a_t = a_vec * e_exc
    r_t = r * e_inc
    k_h = k2 * e_neg
    b_h = b_vec * e_neg
    k_d = k2 * e_end
    b_d = b_vec * e_end
    d_end = jnp.exp(cum_end)

    row64 = lax.broadcasted_iota(jnp.int32, (c, LANES), 0)
    col64 = lax.broadcasted_iota(jnp.int32, (c, LANES), 1) % n
    strict = row64 > col64
    incl = row64 >= col64
    rb = lax.broadcasted_iota(jnp.int32, (LANES, LANES), 0) < n
    cb = lax.broadcasted_iota(jnp.int32, (LANES, LANES), 1) < n
    blockdiag = rb == cb

    def stack_heads(zz, mask):
        return jnp.concatenate([jnp.where(mask, zz, 0.0), jnp.where(mask, 0.0, zz)],
                               axis=0).astype(BF16)

    qs = range(n_chunks)
    ch = lambda t: [t[q * c:(q + 1) * c] for q in qs]
    at_c, rt_c, kh_c, bh_c, kd_c, bd_c, v_c = (ch(t) for t in (a_t, r_t, k_h, b_h, k_d, b_d, v))
    out0 = [_nt_dot(jnp.concatenate([jnp.where(m0, at_c[q], 0.0), jnp.where(m0, rt_c[q], 0.0)],
                                    axis=0).astype(BF16),
                    jnp.concatenate([bh_c[q], kh_c[q]], axis=0).astype(BF16)) for q in qs]
    out1 = [_nt_dot(jnp.concatenate([jnp.where(m0, 0.0, at_c[q]), jnp.where(m0, 0.0, rt_c[q])],
                                    axis=0).astype(BF16),
                    jnp.concatenate([kh_c[q], bh_c[q]], axis=0).astype(BF16)) for q in qs]
    x_cat = [jnp.where(strict, jnp.where(m0, out0[q][:c], out1[q][:c]), 0.0) for q in qs]
    ak_x = [jnp.where(strict, jnp.where(m0, out1[q][:c], out0[q][:c]), 0.0) for q in qs]
    rb_cat = [jnp.where(incl, jnp.where(m0, out0[q][c:], out1[q][c:]), 0.0) for q in qs]
    rk_x = [jnp.where(incl, jnp.where(m0, out1[q][c:], out0[q][c:]), 0.0) for q in qs]
    w12 = [_dot(jnp.concatenate([ak_x[q], rk_x[q]], axis=0).astype(BF16),
                jnp.concatenate([jnp.where(m0, 0.0, v_c[q]), jnp.where(m0, v_c[q], 0.0)],
                                axis=0).astype(BF16)) for q in qs]

    zz = [jnp.concatenate([at_c[q], w12[q][:c]], axis=1) for q in qs]
    xp = x_cat
    for step in range(6):
        zz = [zz[q] + _dot(xp[q].astype(BF16), stack_heads(zz[q], m0_256)) for q in qs]
        if step < 5:
            xp = [_dot(xp[q].astype(BF16), stack_heads(xp[q], m0)) for q in qs]
    rbz = [_dot(rb_cat[q].astype(BF16), stack_heads(zz[q], m0_256)) for q in qs]
    q_c = [rt_c[q] + rbz[q][:, 0:LANES] for q in qs]
    z_c = [w12[q][c:] + rbz[q][:, LANES:] for q in qs]
    mn = []
    for q in qs:
        t_src = jnp.concatenate(
            [zz[q], jnp.concatenate([jnp.zeros_like(v_c[q]), v_c[q]], axis=1)], axis=0)
        mn.append(_dot(t_src.T.astype(BF16),
                       jnp.concatenate([bd_c[q], kd_c[q]], axis=0).astype(BF16)))
    m_x = [jnp.where(blockdiag, mn[q][0:LANES], 0.0).astype(BF16) for q in qs]
    n_x = [jnp.where(blockdiag, mn[q][LANES:], 0.0) for q in qs]

    state = st_ref[...]
    ys = []
    for q in qs:
        s_bf = state.astype(BF16)
        ys.append(_nt_dot(q_c[q].astype(BF16), s_bf) + z_c[q])
        state = state * d_end[q * c:q * c + 1, :] + _dot(s_bf, m_x[q]) + n_x[q]

    st_ref[...] = state
    y = jnp.concatenate(ys, axis=0)
    mu = head_sum(y) * (1.0 / n)
    yc = y - mu
    var = head_sum(yc * yc) * (1.0 / n)
    yn = yc * lax.rsqrt(var + RWKV_GN_EPS) * gn_w + gn_b
    o_ref[0] = ((yn + bonus) * g).astype(o_ref.dtype)


def _rwkv(proj3, wd, wa, wg, prm, seq):
    bsz = proj3.shape[0]
    tm = min(1024, seq)
    n_pairs = RWKV_WIDTH // LANES
    blk = lambda off: (lambda b, p, t: (b, t, off // LANES + p))
    kern = functools.partial(_rwkv_kernel, tm=tm)
    return pl.pallas_call(
        kern,
        grid=(bsz, n_pairs, seq // tm),
        in_specs=[pl.BlockSpec((1, tm, LANES), blk(OFF_R)),
                  pl.BlockSpec((1, tm, LANES), blk(OFF_K)),
                  pl.BlockSpec((1, tm, LANES), blk(OFF_V)),
                  pl.BlockSpec((1, tm, LORA_PACK), lambda b, p, t: (b, t, OFF_LORA // LORA_PACK)),
                  pl.BlockSpec((LANES, LANES), lambda b, p, t: (0, p)),
                  pl.BlockSpec((LANES, LANES), lambda b, p, t: (0, p)),
                  pl.BlockSpec((2 * LANES, LANES), lambda b, p, t: (0, p)),
                  pl.BlockSpec((8, LANES), lambda b, p, t: (0, p))],
        out_specs=pl.BlockSpec((1, tm, LANES), lambda b, p, t: (b, t, p)),
        out_shape=jax.ShapeDtypeStruct((bsz, seq, RWKV_WIDTH), BF16),
        scratch_shapes=[pltpu.VMEM((LANES, LANES), F32)],
        compiler_params=_cparams(("arbitrary", "arbitrary", "arbitrary")),
        name="rwkv",
    )(proj3, proj3, proj3, proj3, wd, wa, wg, prm)


def _attn_prep_kernel(q_ref, k_ref, v_ref, cos_ref, sa_ref, sb_ref, gavg_ref, qg_ref, kg_ref,
                      q2_ref, ko_ref, vo_ref):
    width = q_ref.shape[-1]
    reps = width // LANES
    tile = lambda t: jnp.concatenate([t] * reps, axis=1)
    cos_t, sin_a, sin_b = tile(cos_ref[...]), tile(sa_ref[...]), tile(sb_ref[...])
    gavg = gavg_ref[...]
    half = ROPE_DIM // 2

    def norm_rope(x, gain):
        x = x.astype(F32)
        sq = x * x
        hi = sq.astype(BF16)
        lo = (sq - hi.astype(F32)).astype(BF16)
        ms = _dot(hi, gavg) + _dot(lo, gavg)
        y = x * lax.rsqrt(ms + NORM_EPS) * gain
        return (y * cos_t + pltpu.roll(y, width - half, 1) * sin_a + pltpu.roll(y, half, 1) * sin_b)

    qn = norm_rope(q_ref[0], qg_ref[...])
    kn = norm_rope(k_ref[0], kg_ref[...])
    lane = lax.broadcasted_iota(jnp.int32, (1, LANES), 1)
    m0 = lane < DIFF_QK_DIM
    for h in range(DIFF_HEADS):
        sl = slice(h * LANES, (h + 1) * LANES)
        qh = qn[:, sl]
        q2_ref[0, h, 0] = jnp.where(m0, qh, 0.0).astype(BF16)
        q2_ref[0, h, 1] = jnp.where(m0, 0.0, qh).astype(BF16)
        ko_ref[0, h] = kn[:, sl].astype(BF16)
        vo_ref[0, h] = v_ref[0, :, sl]


def _rope_tables(seq):
    half = ROPE_DIM // 2
    inv_freq = ROPE_THETA ** (-jnp.arange(half, dtype=F32) * 2.0 / ROPE_DIM)
    ang = jnp.arange(seq, dtype=jnp.int32).astype(F32)[:, None] * inv_freq[None, :]
    cos, sin = jnp.cos(ang), jnp.sin(ang)
    ones = jnp.ones((seq, DIFF_QK_DIM - ROPE_DIM), F32)
    zeros = jnp.zeros((seq, DIFF_QK_DIM - ROPE_DIM), F32)
    zh = jnp.zeros((seq, half), F32)
    cos64 = jnp.concatenate([cos, cos, ones], axis=1)
    sa64 = jnp.concatenate([-sin, zh, zeros], axis=1)
    sb64 = jnp.concatenate([zh, sin, zeros], axis=1)
    dup = lambda t: jnp.concatenate([t, t], axis=1)
    return dup(cos64), dup(sa64), dup(sb64)


def _attn_prep(proj3, q_gain, k_gain, seq):
    bsz = proj3.shape[0]
    tm = min(512, seq)
    width = DIFF_HEADS * 2 * DIFF_QK_DIM
    cos_t, sin_a, sin_b = _rope_tables(seq)
    gi = jnp.arange(width) // DIFF_QK_DIM
    gavg = jnp.where(gi[:, None] == gi[None, :], 1.0 / DIFF_QK_DIM, 0.0).astype(BF16)
    reps = width // DIFF_QK_DIM
    qg = (jnp.tile(q_gain, reps) * (DIFF_QK_DIM ** -0.5 * math.log2(math.e))).reshape(1, width)
    kg = jnp.tile(k_gain, reps).reshape(1, width)
    hd = (bsz, DIFF_HEADS, seq, LANES)
    tab = pl.BlockSpec((tm, LANES), lambda b, t: (t, 0))
    return pl.pallas_call(
        _attn_prep_kernel,
        grid=(bsz, seq // tm),
        in_specs=[pl.BlockSpec((1, tm, width), lambda b, t: (b, t, OFF_QD // width)),
                  pl.BlockSpec((1, tm, width), lambda b, t: (b, t, OFF_KD // width)),
                  pl.BlockSpec((1, tm, width), lambda b, t: (b, t, OFF_VD // width)),
                  tab, tab, tab,
                  pl.BlockSpec((width, width), lambda b, t: (0, 0)),
                  pl.BlockSpec((1, width), lambda b, t: (0, 0)),
                  pl.BlockSpec((1, width), lambda b, t: (0, 0))],
        out_specs=[pl.BlockSpec((1, DIFF_HEADS, 2, tm, LANES), lambda b, t: (b, 0, 0, t, 0)),
                   pl.BlockSpec((1, DIFF_HEADS, tm, LANES), lambda b, t: (b, 0, t, 0)),
                   pl.BlockSpec((1, DIFF_HEADS, tm, LANES), lambda b, t: (b, 0, t, 0))],
        out_shape=[jax.ShapeDtypeStruct((bsz, DIFF_HEADS, 2, seq, LANES), BF16),
                   jax.ShapeDtypeStruct(hd, BF16), jax.ShapeDtypeStruct(hd, BF16)],
        compiler_params=_cparams(("arbitrary", "arbitrary")),
        name="attn_prep",
    )(proj3, proj3, proj3, cos_t, sin_a, sin_b, gavg, qg, kg)


def _flash_kernel(qi_ref, kj_ref, q_ref, k_ref, v_ref, lam_ref, sg_ref, o_ref, m_ref, acc_ref,
                  *, tq, tk, lambda_init):
    pidx = pl.program_id(2)
    i = qi_ref[pidx]
    j = kj_ref[pidx]

    @pl.when(j == 0)
    def _():
        m_ref[...] = jnp.full(m_ref.shape, -jnp.inf, F32)
        acc_ref[...] = jnp.zeros(acc_ref.shape, F32)

    q = q_ref[0, 0].reshape(2 * tq, LANES)
    s = _nt_dot(q, k_ref[0, 0]).astype(BF16)

    def update(sc):
        m_old = m_ref[...]
        m_new = jnp.maximum(m_old, jnp.max(sc, axis=-1, keepdims=True).astype(F32))
        alpha = jnp.exp2(m_old - m_new)
        m_b = m_new.astype(BF16)
        p = jnp.exp2(sc - jnp.concatenate([m_b] * (tk // LANES), axis=1))
        v = v_ref[0, 0]
        v_ext = jnp.concatenate([v, jnp.ones_like(v)], axis=1)
        acc_ref[...] = jnp.concatenate([alpha, alpha], axis=1) * acc_ref[...] + _dot(p, v_ext)
        m_ref[...] = m_new

    @pl.when(j < i)
    def _():
        update(s)

    @pl.when(j == i)
    def _():
        qpos = lax.broadcasted_iota(jnp.int32, s.shape, 0) % tq
        kpos = lax.broadcasted_iota(jnp.int32, s.shape, 1)
        update(jnp.where(kpos // CHUNK <= qpos // CHUNK, s, -jnp.inf).astype(BF16))
        lam_rows = lam_ref[...]
        lam = (jnp.exp(jnp.sum(lam_rows[0:1] * lam_rows[1:2], axis=-1, keepdims=True))
               - jnp.exp(jnp.sum(lam_rows[2:3] * lam_rows[3:4], axis=-1, keepdims=True))
               + lambda_init)
        acc = acc_ref[...]
        o1 = acc[0:tq, 0:LANES] / acc[0:tq, LANES:]
        o2 = acc[tq:, 0:LANES] / acc[tq:, LANES:]
        o = o1 - lam * o2
        ms = jnp.mean(o * o, axis=-1, keepdims=True)
        o_ref[0] = (o * lax.rsqrt(ms + NORM_EPS) * sg_ref[...] * (1.0 - lambda_init)).astype(o_ref.dtype)


def _flash(q2, kh, vh, lam_rows, subln_gain, seq, lambda_init):
    bsz = q2.shape[0]
    tq = tk = min(1024, seq)
    nq = seq // tq
    pairs = [(i, j) for i in range(nq) for j in range(i + 1)]
    qi = jnp.asarray([p[0] for p in pairs], jnp.int32)
    kj = jnp.asarray([p[1] for p in pairs], jnp.int32)
    kern = functools.partial(_flash_kernel, tq=tq, tk=tk, lambda_init=lambda_init)
    grid_spec = pltpu.PrefetchScalarGridSpec(
        num_scalar_prefetch=2,
        grid=(bsz, DIFF_HEADS, len(pairs)),
        in_specs=[pl.BlockSpec((1, 1, 2, tq, LANES), lambda b, h, p, qi, kj: (b, h, 0, qi[p], 0)),
                  pl.BlockSpec((1, 1, tk, LANES), lambda b, h, p, qi, kj: (b, h, kj[p], 0)),
                  pl.BlockSpec((1, 1, tk, LANES), lambda b, h, p, qi, kj: (b, h, kj[p], 0)),
                  pl.BlockSpec((8, LANES), lambda b, h, p, qi, kj: (0, 0)),
                  pl.BlockSpec((1, LANES), lambda b, h, p, qi, kj: (0, 0))],
        out_specs=pl.BlockSpec((1, tq, LANES), lambda b, h, p, qi, kj: (b, qi[p], h)),
        scratch_shapes=[pltpu.VMEM((2 * tq, LANES), F32), pltpu.VMEM((2 * tq, 2 * LANES), F32)],
    )
    return pl.pallas_call(
        kern,
        grid_spec=grid_spec,
        out_shape=jax.ShapeDtypeStruct((bsz, seq, DIFF_WIDTH), BF16),
        compiler_params=_cparams(("arbitrary", "arbitrary", "arbitrary")),
        name="flash",
    )(qi, kj, q2, kh, vh, lam_rows, subln_gain.reshape(1, LANES))


def _merge_kernel(x_ref, oa_ref, ob_ref, ga_ref, gb_ref, wa_ref, wb_ref, wo_ref, g1_ref,
                  n2_ref, sc2_ref, sh2_ref, wr2_ref, wrh_ref, br_ref,
                  x1_ref, h2_ref, idx_ref, tw_ref, *, n_experts):
    ma = _dot(oa_ref[...], wa_ref[...])
    mb = _dot(ob_ref[...], wb_ref[...])
    merged = (jax.nn.sigmoid(ga_ref[...].astype(F32)) * ma
              + jax.nn.sigmoid(gb_ref[...].astype(F32)) * mb)
    x1 = x_ref[...] + g1_ref[0] * _dot(merged.astype(BF16), wo_ref[...])
    x1_ref[...] = x1
    ms = jnp.mean(x1 * x1, axis=-1, keepdims=True)
    h2 = x1 * lax.rsqrt(ms + NORM_EPS) * n2_ref[...] * (1.0 + sc2_ref[0]) + sh2_ref[0]
    h2_ref[...] = h2

    h_hi = h2.astype(BF16)
    h_lo = (h2 - h_hi.astype(F32)).astype(BF16)
    hh = _dot(h_hi, wr2_ref[...])
    logits = hh[:, 0:LANES] + hh[:, LANES:] + _dot(h_lo, wrh_ref[...]) + br_ref[...]
    lane = lax.broadcasted_iota(jnp.int32, logits.shape, 1)
    lane_f = lane.astype(F32)
    lg = jnp.where(lane < n_experts, logits, -jnp.inf)
    vals, idxs = [], []
    for _ in range(TOP_K):
        mx = jnp.max(lg, axis=-1, keepdims=True)
        ix = jnp.min(jnp.where(lg == mx, lane_f, float(LANES)), axis=-1, keepdims=True)
        vals.append(mx)
        idxs.append(ix)
        lg = jnp.where(lane_f == ix, -jnp.inf, lg)
    exps = [jnp.exp(vv - vals[0]) for vv in vals]
    denom = exps[0] + exps[1] + exps[2] + exps[3]
    idx_out = jnp.zeros(logits.shape, F32)
    tw_out = jnp.zeros(logits.shape, F32)
    for kk in range(TOP_K):
        idx_out = jnp.where(lane == kk, idxs[kk], idx_out)
        tw_out = jnp.where(lane == kk, exps[kk] / denom, tw_out)
    idx_ref[...] = idx_out.astype(jnp.int32)
    tw_ref[...] = tw_out


def _merge(x2, oa, ob, proj, wa, wb, wo, g1, n2, sc2, sh2, wr, br, seq, n_experts):
    t, d = x2.shape
    tm = min(256, seq)
    tpb = seq // tm
    gw = d
    wr_hi = wr.astype(BF16)
    wr_lo = (wr - wr_hi.astype(F32)).astype(BF16)
    wr2 = jnp.concatenate([wr_hi, wr_lo], axis=1)
    const = lambda shape: pl.BlockSpec(shape, lambda i: tuple(0 for _ in shape),
                                       pipeline_mode=pl.Buffered(1))
    bvec = pl.BlockSpec((1, 1, d), lambda i: (i // tpb, 0, 0))
    kern = functools.partial(_merge_kernel, n_experts=n_experts)
    return pl.pallas_call(
        kern,
        grid=(t // tm,),
        in_specs=[pl.BlockSpec((tm, d), lambda i: (i, 0)),
                  pl.BlockSpec((tm, RWKV_WIDTH), lambda i: (i, 0)),
                  pl.BlockSpec((tm, DIFF_WIDTH), lambda i: (i, 0)),
                  pl.BlockSpec((tm, gw), lambda i: (i, OFF_GA // gw)),
                  pl.BlockSpec((tm, gw), lambda i: (i, OFF_GB // gw)),
                  const(wa.shape), const(wb.shape), const(wo.shape),
                  bvec, const((1, d)), bvec, bvec,
                  const(wr2.shape), const(wr_hi.shape), const(br.shape)],
        out_specs=[pl.BlockSpec((tm, d), lambda i: (i, 0)),
                   pl.BlockSpec((tm, d), lambda i: (i, 0)),
                   pl.BlockSpec((tm, LANES), lambda i: (i, 0)),
                   pl.BlockSpec((tm, LANES), lambda i: (i, 0))],
        out_shape=[jax.ShapeDtypeStruct((t, d), F32), jax.ShapeDtypeStruct((t, d), F32),
                   jax.ShapeDtypeStruct((t, LANES), jnp.int32),
                   jax.ShapeDtypeStruct((t, LANES), F32)],
        compiler_params=_cparams(("arbitrary",)),
        name="merge",
    )(x2, oa, ob, proj, proj, wa, wb, wo, g1, n2, sc2, sh2, wr2, wr_hi, br)


def _row_copy(src_hbm, dst_ref, src_row, dst_row, sem):
    return pltpu.make_async_copy(src_hbm.at[pl.ds(src_row, 1)], dst_ref.at[pl.ds(dst_row, 1)], sem)


def _expert_kernel(be_ref, na_ref, tok0_ref, tokn_ref, h_hbm, wg_ref, bg_ref, wu_ref, bu_ref,
                   wd_ref, bd_ref, ys_ref, xs_scr, xb_scr, acc_scr, sems, *, bm, rows_per_step):
    i = pl.program_id(0)
    f = pl.program_id(1)
    n_act = na_ref[0]
    slot = i % 2

    def wait_slot(s):
        pltpu.make_async_copy(xs_scr.at[s], xs_scr.at[s], sems.at[s]).wait()

    @pl.when(i < n_act)
    def _():
        @pl.when(f == 0)
        def _():
            @pl.when(i == 0)
            def _():
                def issue(r, carry):
                    _row_copy(h_hbm, xs_scr.at[0], tok0_ref[0, 0, r], r, sems.at[0]).start()
                    return carry

                lax.fori_loop(0, bm, issue, 0)

            wait_slot(slot)
            xb_scr[...] = xs_scr[slot].astype(BF16)
            acc_scr[...] = jnp.zeros(acc_scr.shape, F32)

        for r in range(rows_per_step):
            _row_copy(h_hbm, xs_scr.at[1 - slot], tokn_ref[0, 0, r], f * rows_per_step + r,
                      sems.at[1 - slot]).start()

        xb = xb_scr[...]
        gt = jnp.minimum(_dot(xb, wg_ref[0].astype(BF16)) + bg_ref[0], SWIGLU_LIMIT)
        up = jnp.clip(_dot(xb, wu_ref[0].astype(BF16)) + bu_ref[0], -SWIGLU_LIMIT, SWIGLU_LIMIT)
        act = (up + 1.0) * gt * jax.nn.sigmoid(SWIGLU_ALPHA * gt)
        acc_scr[...] += _dot(act.astype(BF16), wd_ref[0].astype(BF16))

        @pl.when(f == pl.num_programs(1) - 1)
        def _():
            ys_ref[...] = acc_scr[...] + bd_ref[0]

            @pl.when(i == n_act - 1)
            def _():
                wait_slot(1 - slot)

    @pl.when(jnp.logical_and(i >= n_act, f == 0))
    def _():
        ys_ref[...] = jnp.zeros(ys_ref.shape, F32)


def _experts(h2, row_token, block_expert, n_active, wg, bg, wu, bu, wd, bd, bm):
    d = h2.shape[1]
    n_exp, _, ff = wg.shape
    tf = 512
    nf = ff // tf
    rows_per_step = bm // nf
    p = row_token.shape[0]
    n_blocks = p // bm

    def row(i, na):
        return jnp.minimum(i, na[0] - 1)

    def next_rows(i, f, be, na):
        return (jnp.minimum(i + 1, n_blocks - 1) * nf + f, 0, 0)

    kern = functools.partial(_expert_kernel, bm=bm, rows_per_step=rows_per_step)
    grid_spec = pltpu.PrefetchScalarGridSpec(
        num_scalar_prefetch=2,
        grid=(n_blocks, nf),
        in_specs=[pl.BlockSpec((1, 1, bm), lambda i, f, be, na: (0, 0, 0), memory_space=pltpu.SMEM),
                  pl.BlockSpec((1, 1, rows_per_step), next_rows, memory_space=pltpu.SMEM),
                  pl.BlockSpec(memory_space=pl.ANY),
                  pl.BlockSpec((1, d, tf), lambda i, f, be, na: (be[row(i, na)], 0, f)),
                  pl.BlockSpec((1, 1, tf), lambda i, f, be, na: (be[row(i, na)], 0, f)),
                  pl.BlockSpec((1, d, tf), lambda i, f, be, na: (be[row(i, na)], 0, f)),
                  pl.BlockSpec((1, 1, tf), lambda i, f, be, na: (be[row(i, na)], 0, f)),
                  pl.BlockSpec((1, tf, d), lambda i, f, be, na: (be[row(i, na)], f, 0)),
                  pl.BlockSpec((1, 1, d), lambda i, f, be, na: (be[row(i, na)], 0, 0))],
        out_specs=pl.BlockSpec((bm, d), lambda i, f, be, na: (i, 0)),
        scratch_shapes=[pltpu.VMEM((2, bm, d), F32), pltpu.VMEM((bm, d), BF16),
                        pltpu.VMEM((bm, d), F32), pltpu.SemaphoreType.DMA((2,))],
    )
    return pl.pallas_call(
        kern,
        grid_spec=grid_spec,
        out_shape=jax.ShapeDtypeStruct((p, d), F32),
        compiler_params=_cparams(("arbitrary", "arbitrary")),
        name="experts",
    )(block_expert, n_active, row_token.reshape(n_blocks, 1, bm),
      row_token.reshape(n_blocks * nf, 1, rows_per_step), h2,
      wg, bg.reshape(n_exp, 1, ff), wu, bu.reshape(n_exp, 1, ff), wd, bd.reshape(n_exp, 1, d))


def _combine_kernel(pos_ref, x1_ref, tw_ref, g2_ref, ys_hbm, o_ref, rows_scr, sem, *, tc):
    def issue(r, carry):
        for kk in range(TOP_K):
            _row_copy(ys_hbm, rows_scr.at[kk], pos_ref[0, 0, r * TOP_K + kk], r, sem).start(
                priority=kk % 2)
        return carry

    lax.fori_loop(0, tc, issue, 0)
    pltpu.make_async_copy(rows_scr, rows_scr, sem).wait()
    tw = tw_ref[...]
    moe = tw[:, 0:1] * rows_scr[0]
    for kk in range(1, TOP_K):
        moe = moe + tw[:, kk:kk + 1] * rows_scr[kk]
    o_ref[...] = x1_ref[...] + g2_ref[0] * moe


def _combine(x1, top_w, g2, ys, pos, seq):
    t, d = x1.shape
    tc = min(128, seq)
    tpb = seq // tc
    n_steps = t // tc
    kern = functools.partial(_combine_kernel, tc=tc)
    return pl.pallas_call(
        kern,
        grid=(n_steps,),
        in_specs=[pl.BlockSpec((1, 1, tc * TOP_K), lambda i: (i, 0, 0), memory_space=pltpu.SMEM),
                  pl.BlockSpec((tc, d), lambda i: (i, 0)),
                  pl.BlockSpec((tc, LANES), lambda i: (i, 0)),
                  pl.BlockSpec((1, 1, d), lambda i: (i // tpb, 0, 0)),
                  pl.BlockSpec(memory_space=pl.ANY)],
        out_specs=pl.BlockSpec((tc, d), lambda i: (i, 0)),
        out_shape=jax.ShapeDtypeStruct((t, d), F32),
        scratch_shapes=[pltpu.VMEM((TOP_K, tc, d), F32), pltpu.SemaphoreType.DMA(())],
        compiler_params=_cparams(("arbitrary",)),
        name="combine",
    )(pos.reshape(n_steps, 1, tc * TOP_K), x1, top_w, g2, ys)


def _route(top_idx, n_experts, bm):
    t = top_idx.shape[0]
    m = t * TOP_K
    n_blocks = (m + n_experts * (bm - 1) + bm - 1) // bm
    flat_e = top_idx.reshape(m)
    onehot = (flat_e[:, None] == jnp.arange(n_experts, dtype=jnp.int32)[None, :]).astype(jnp.int32)
    csum = jnp.cumsum(onehot, axis=0)
    rank = jnp.take_along_axis(csum, flat_e[:, None], axis=1)[:, 0] - 1
    counts = csum[-1]
    padded = (counts + bm - 1) // bm * bm
    pad_end = jnp.cumsum(padded)
    pad_start = pad_end - padded
    dest = (pad_start[flat_e] + rank).astype(jnp.int32)
    row_token = jnp.zeros((n_blocks * bm,), jnp.int32).at[dest].set(
        jnp.arange(m, dtype=jnp.int32) // TOP_K)
    block_start = jnp.arange(n_blocks, dtype=jnp.int32) * bm
    block_expert = jnp.minimum(jnp.searchsorted(pad_end, block_start, side='right'),
                               n_experts - 1).astype(jnp.int32)
    n_active = (pad_end[-1] // bm).astype(jnp.int32).reshape(1)
    return row_token, dest, block_expert, n_active


def _moe(x1, h2, top_idx, top_w, g2, wg, bg, wu, bu, wd, bd, seq):
    n_experts = wg.shape[0]
    bm = 512
    row_token, dest, block_expert, n_active = _route(top_idx[:, :TOP_K], n_experts, bm)
    ys = _experts(h2, row_token, block_expert, n_active, wg, bg, wu, bu, wd, bd, bm)
    return _combine(x1, top_w, g2, ys, dest, seq)


def _layer(x, c, l, w_ada, b_ada, norm1_gain, norm2_gain, w_in, shift_mu, w0, w_decay_up, a0,
           w_aaa_up, w_gate_up, k_k, k_a, r_k, gn_w, gn_b, q_gain, k_gain, lam_q1, lam_k1,
           lam_q2, lam_k2, subln_gain, w_branch_a, w_branch_b, w_out, w_router, b_router,
           w_exp_gate, b_exp_gate, w_exp_up, b_exp_up, w_exp_down, b_exp_down):
    bsz, seq, d = x.shape
    t = bsz * seq
    mod = _ada(c, w_ada, b_ada)
    sh1, sc1, g1, sh2, sc2, g2 = (m.reshape(bsz, 1, d) for m in jnp.split(mod, 6, axis=-1))

    decay_rank, aaa_rank, gate_rank = w_decay_up.shape[0], w_aaa_up.shape[0], w_gate_up.shape[0]
    w_packed, mu_packed = _pack_in_weights(w_in, shift_mu, decay_rank, aaa_rank, gate_rank)
    x2 = x.reshape(t, d)
    proj = _inproj(x2, norm1_gain.reshape(1, d), sc1, sh1, w_packed, mu_packed, seq)
    proj3 = proj.reshape(bsz, seq, PACKED_WIDTH)

    padr = lambda w: jnp.pad(w, ((0, LANES - w.shape[0]), (0, 0))).astype(BF16)
    prm = jnp.stack([w0, a0, k_k, k_a, r_k, gn_w, gn_b, jnp.zeros_like(w0)], axis=0)
    o_a = _rwkv(proj3, padr(w_decay_up), padr(w_aaa_up), w_gate_up.astype(BF16), prm, seq)

    lambda_init = 0.8 - 0.6 * math.exp(-0.3 * l)
    q2, kh, vh = _attn_prep(proj3, q_gain, k_gain, seq)
    lam_rows = jnp.pad(jnp.stack([lam_q1, lam_k1, lam_q2, lam_k2], axis=0),
                       ((0, 4), (0, LANES - DIFF_QK_DIM)))
    o_b = _flash(q2, kh, vh, lam_rows, subln_gain, seq, lambda_init)

    n_experts = w_router.shape[1]
    wr = jnp.pad(w_router, ((0, 0), (0, LANES - n_experts)))
    br = jnp.pad(b_router, (0, LANES - n_experts)).reshape(1, LANES)
    x1, h2, top_idx, top_w = _merge(
        x2, o_a.reshape(t, RWKV_WIDTH), o_b.reshape(t, DIFF_WIDTH), proj,
        w_branch_a.astype(BF16), w_branch_b.astype(BF16), w_out.astype(BF16),
        g1, norm2_gain.reshape(1, d), sc2, sh2, wr, br, seq, n_experts)

    out = _moe(x1, h2, top_idx, top_w, g2, w_exp_gate, b_exp_gate, w_exp_up, b_exp_up,
               w_exp_down, b_exp_down, seq)
    return out.reshape(bsz, seq, d)


def kernel(x, c, w_ada, b_ada, norm1_gain, norm2_gain, w_in, shift_mu, w0, w_decay_up, a0, w_aaa_up, w_gate_up, k_k, k_a, r_k, gn_w, gn_b, q_gain, k_gain, lam_q1, lam_k1, lam_q2, lam_k2, subln_gain, w_branch_a, w_branch_b, w_out, w_router, b_router, w_exp_gate, b_exp_gate, w_exp_up, b_exp_up, w_exp_down, b_exp_down):
    per_layer = (w_ada, b_ada, norm1_gain, norm2_gain, w_in, shift_mu, w0, w_decay_up, a0,
                 w_aaa_up, w_gate_up, k_k, k_a, r_k, gn_w, gn_b, q_gain, k_gain, lam_q1, lam_k1,
                 lam_q2, lam_k2, subln_gain, w_branch_a, w_branch_b, w_out, w_router, b_router,
                 w_exp_gate, b_exp_gate, w_exp_up, b_exp_up, w_exp_down, b_exp_down)
    for l in range(w_in.shape[0]):
        x = _layer(x, c, l, *(p[l] for p in per_layer))
    return x
```

```python
import functools
import math

import jax
import jax.numpy as jnp
from jax import lax
from jax.experimental import pallas as pl
from jax.experimental.pallas import tpu as pltpu

F32 = jnp.float32
BF16 = jnp.bfloat16

NORM_EPS = 1e-6
CHUNK = 64

RWKV_HEADS = 16
RWKV_HEAD_DIM = 64
RWKV_WIDTH = RWKV_HEADS * RWKV_HEAD_DIM
RWKV_GN_EPS = 64e-5

DIFF_HEADS = 8
DIFF_QK_DIM = 64
DIFF_V_DIM = 128
DIFF_WIDTH = DIFF_HEADS * DIFF_V_DIM
ROPE_THETA = 500000.0
ROPE_DIM = DIFF_QK_DIM // 4

TOP_K = 4
SWIGLU_LIMIT = 7.0
SWIGLU_ALPHA = 1.702

LANES = 128
LORA_PACK = 512
VMEM_LIMIT = 56 * 1024 * 1024

OFF_R, OFF_K, OFF_V = 0, 1024, 2048
OFF_QD, OFF_KD, OFF_VD = 3072, 4096, 5120
OFF_GA, OFF_GB = 6144, 8192
OFF_LORA = 10240
PACKED_WIDTH = OFF_LORA + LORA_PACK


def _cparams(sem):
    return pltpu.CompilerParams(dimension_semantics=sem, vmem_limit_bytes=VMEM_LIMIT)


def _nt_dot(a, b):
    return lax.dot_general(a, b, (((1,), (1,)), ((), ())), preferred_element_type=F32)


def _dot(a, b):
    return jnp.dot(a, b, preferred_element_type=F32)


def _ada_kernel(c_ref, w_ref, b_ref, o_ref):
    c = c_ref[...]
    s = c * jax.nn.sigmoid(c)
    o_ref[...] = _dot(s.astype(BF16), w_ref[...].astype(BF16)) + b_ref[...]


def _ada(c, w, b):
    bsz, d = c.shape
    n = w.shape[1]
    rows = 8
    tn = 1024
    cp = jnp.pad(c, ((0, rows - bsz), (0, 0)))
    out = pl.pallas_call(
        _ada_kernel,
        grid=(n // tn,),
        in_specs=[pl.BlockSpec((rows, d), lambda j: (0, 0)),
                  pl.BlockSpec((d, tn), lambda j: (0, j)),
                  pl.BlockSpec((1, tn), lambda j: (0, j))],
        out_specs=pl.BlockSpec((rows, tn), lambda j: (0, j)),
        out_shape=jax.ShapeDtypeStruct((rows, n), F32),
        compiler_params=_cparams(("arbitrary",)),
        name="ada",
    )(cp, w, b.reshape(1, n))
    return out[:bsz]


def _inproj_kernel(x_ref, g_ref, sc_ref, sh_ref, w_ref, mu_ref, o_ref, h_scr, carry_scr,
                   *, tiles_per_batch, n_rkv_tiles, lora_tile):
    i = pl.program_id(0)
    j = pl.program_id(1)

    @pl.when(j == 0)
    def _():
        x = x_ref[...]
        ms = jnp.mean(x * x, axis=-1, keepdims=True)
        y = x * lax.rsqrt(ms + NORM_EPS) * g_ref[...]
        h_scr[...] = (y * (1.0 + sc_ref[0]) + sh_ref[0]).astype(BF16)

    p = _dot(h_scr[...], w_ref[...])
    shifted = jnp.logical_or(j < n_rkv_tiles, j == lora_tile)

    @pl.when(shifted)
    def _():
        slot = jnp.where(j == lora_tile, n_rkv_tiles, j)

        @pl.when(i % tiles_per_batch == 0)
        def _():
            carry_scr[slot] = jnp.zeros(carry_scr.shape[1:], F32)

        prev_last = carry_scr[slot]
        tm = p.shape[0]
        carry_scr[slot] = p[tm - 1:tm, :]
        row = lax.broadcasted_iota(jnp.int32, p.shape, 0)
        prev = jnp.where(row == 0, prev_last, pltpu.roll(p, 1, 0))
        o_ref[...] = (p + (prev - p) * mu_ref[...]).astype(o_ref.dtype)

    @pl.when(jnp.logical_not(shifted))
    def _():
        o_ref[...] = p.astype(o_ref.dtype)


def _inproj(x2, gain, sc, sh, w_packed, mu_packed, seq):
    t, d = x2.shape
    tm = min(1024, seq)
    tn = 1536
    assert PACKED_WIDTH % tn == 0 and OFF_QD % tn == 0 and PACKED_WIDTH - tn <= OFF_LORA
    n_col = PACKED_WIDTH // tn
    n_rkv_tiles = OFF_QD // tn
    lora_tile = n_col - 1
    tiles_per_batch = seq // tm

    kern = functools.partial(_inproj_kernel, tiles_per_batch=tiles_per_batch,
                             n_rkv_tiles=n_rkv_tiles, lora_tile=lora_tile)
    return pl.pallas_call(
        kern,
        grid=(t // tm, n_col),
        in_specs=[pl.BlockSpec((tm, d), lambda i, j: (i, 0)),
                  pl.BlockSpec((1, d), lambda i, j: (0, 0)),
                  pl.BlockSpec((1, 1, d), lambda i, j: (i // tiles_per_batch, 0, 0)),
                  pl.BlockSpec((1, 1, d), lambda i, j: (i // tiles_per_batch, 0, 0)),
                  pl.BlockSpec((d, tn), lambda i, j: (0, j)),
                  pl.BlockSpec((1, tn), lambda i, j: (0, j))],
        out_specs=pl.BlockSpec((tm, tn), lambda i, j: (i, j)),
        out_shape=jax.ShapeDtypeStruct((t, PACKED_WIDTH), BF16),
        scratch_shapes=[pltpu.VMEM((tm, d), BF16),
                        pltpu.VMEM((n_rkv_tiles + 1, 1, tn), F32)],
        compiler_params=_cparams(("arbitrary", "arbitrary")),
        name="inproj",
    )(x2, gain, sc, sh, w_packed, mu_packed)


def _pack_in_weights(w_in, shift_mu, decay_rank, aaa_rank, gate_rank):
    assert decay_rank <= LANES and aaa_rank <= LANES and gate_rank == 2 * LANES
    rw = RWKV_WIDTH
    o = 3 * rw
    w_r, w_k, w_v = w_in[:, 0:rw], w_in[:, rw:2 * rw], w_in[:, 2 * rw:3 * rw]
    w_dl = w_in[:, o:o + decay_rank]
    w_al = w_in[:, o + decay_rank:o + decay_rank + aaa_rank]
    w_gl = w_in[:, o + decay_rank + aaa_rank:o + decay_rank + aaa_rank + gate_rank]
    rest = w_in[:, o + decay_rank + aaa_rank + gate_rank:]

    def padc(a, width):
        return jnp.pad(a, ((0, 0), (0, width - a.shape[1])))

    packed = jnp.concatenate(
        [w_r, w_k, w_v, rest, padc(w_dl, LANES), padc(w_al, LANES), w_gl], axis=1).astype(BF16)
    assert packed.shape[1] == PACKED_WIDTH
    mu = shift_mu.reshape(1, -1)
    mu_packed = jnp.concatenate(
        [mu[:, :o], jnp.zeros((1, OFF_LORA - o), F32), padc(mu[:, o:o + decay_rank], LANES),
         padc(mu[:, o + decay_rank:o + decay_rank + aaa_rank], LANES),
         mu[:, o + decay_rank + aaa_rank:]], axis=1)
    return packed, mu_packed


def _split3_bf16(x):
    hi = x.astype(BF16)
    r1 = x - hi.astype(F32)
    mid = r1.astype(BF16)
    lo = (r1 - mid.astype(F32)).astype(BF16)
    return hi, mid, lo


def _rwkv_kernel(r_ref, k_ref, v_ref, lora_ref, wd_ref, wa_ref, wg_ref, prm_ref, o_ref, st_ref,
                 *, tm):
    c = CHUNK
    n = RWKV_HEAD_DIM
    n_chunks = tm // c

    @pl.when(pl.program_id(2) == 0)
    def _():
        st_ref[...] = jnp.zeros(st_ref.shape, F32)

    lane = lax.broadcasted_iota(jnp.int32, (1, LANES), 1)
    m0 = lane < n
    m0_256 = jnp.concatenate([m0, m0], axis=1)

    def head_sum(x):
        s0 = jnp.sum(jnp.where(m0, x, 0.0), axis=-1, keepdims=True)
        s1 = jnp.sum(jnp.where(m0, 0.0, x), axis=-1, keepdims=True)
        return jnp.where(m0, s0, s1)

    prm = prm_ref[...]
    w0, a0, k_k, k_a, r_k, gn_w, gn_b = (prm[i:i + 1] for i in range(7))

    r = r_ref[0].astype(F32)
    k = k_ref[0].astype(F32)
    v = v_ref[0].astype(F32)
    lora = lora_ref[0]
    d_code = lora[:, 0:LANES].astype(F32)
    a_code = lora[:, LANES:2 * LANES]
    g_code = lora[:, 2 * LANES:4 * LANES].astype(F32)

    wl = w0 + _dot(jnp.tanh(d_code).astype(BF16), wd_ref[...])
    z = -wl
    softplus = jnp.maximum(z, 0.0) + jnp.log(1.0 + jnp.exp(-jnp.abs(z)))
    lw = -jnp.exp(-softplus - 0.5)
    a = jax.nn.sigmoid(a0 + _dot(a_code, wa_ref[...]))
    g = _dot(jax.nn.sigmoid(g_code).astype(BF16), wg_ref[...])

    kk = k * k_k
    kk = kk / jnp.maximum(jnp.sqrt(head_sum(kk * kk)), 1e-12)
    k2 = k * (1.0 + (a - 1.0) * k_a)
    a_vec = -kk
    b_vec = kk * a
    bonus = head_sum(r * k2 * r_k) * v

    tb = min(tm, 2 * LANES)
    ri = lax.broadcasted_iota(jnp.int32, (tb, tb), 0)
    ci = lax.broadcasted_iota(jnp.int32, (tb, tb), 1)
    tri = jnp.where(jnp.logical_and(ci <= ri, ci // c == ri // c), 1.0, 0.0).astype(BF16)
    lw3 = jnp.concatenate(_split3_bf16(lw), axis=1)
    cum3 = jnp.concatenate([_dot(tri, lw3[u * tb:(u + 1) * tb]) for u in range(tm // tb)], axis=0)
    cum = cum3[:, 0:LANES] + cum3[:, LANES:2 * LANES] + cum3[:, 2 * LANES:3 * LANES]
    cum_end = jnp.concatenate(
        [jnp.broadcast_to(cum[(q + 1) * c - 1:(q + 1) * c, :], (c, LANES)) for q in range(n_chunks)],
        axis=0)

    e_inc = jnp.exp(cum)
    e_exc = jnp.exp(cum - lw)
    e_neg = jnp.exp(-cum)
    e_end = jnp.exp(cum_end - cum)
    a_t = a_vec * e_exc
    r_t = r * e_inc
    k_h = k2 * e_neg
    b_h = b_vec * e_neg
    k_d = k2 * e_end
    b_d = b_vec * e_end
    d_end = jnp.exp(cum_end)

    row64 = lax.broadcasted_iota(jnp.int32, (c, LANES), 0)
    col64 = lax.broadcasted_iota(jnp.int32, (c, LANES), 1) % n
    strict = row64 > col64
    incl = row64 >= col64
    rb = lax.broadcasted_iota(jnp.int32, (LANES, LANES), 0) < n
    cb = lax.broadcasted_iota(jnp.int32, (LANES, LANES), 1) < n
    blockdiag = rb == cb

    def stack_heads(zz, mask):
        return jnp.concatenate([jnp.where(mask, zz, 0.0), jnp.where(mask, 0.0, zz)],
                               axis=0).astype(BF16)

    qs = range(n_chunks)
    ch = lambda t: [t[q * c:(q + 1) * c] for q in qs]
    at_c, rt_c, kh_c, bh_c, kd_c, bd_c, v_c = (ch(t) for t in (a_t, r_t, k_h, b_h, k_d, b_d, v))
    out0 = [_nt_dot(jnp.concatenate([jnp.where(m0, at_c[q], 0.0), jnp.where(m0, rt_c[q], 0.0)],
                                    axis=0).astype(BF16),
                    jnp.concatenate([bh_c[q], kh_c[q]], axis=0).astype(BF16)) for q in qs]
    out1 = [_nt_dot(jnp.concatenate([jnp.where(m0, 0.0, at_c[q]), jnp.where(m0, 0.0, rt_c[q])],
                                    axis=0).astype(BF16),
                    jnp.concatenate([kh_c[q], bh_c[q]], axis=0).astype(BF16)) for q in qs]
    x_cat = [jnp.where(strict, jnp.where(m0, out0[q][:c], out1[q][:c]), 0.0) for q in qs]
    ak_x = [jnp.where(strict, jnp.where(m0, out1[q][:c], out0[q][:c]), 0.0) for q in qs]
    rb_cat = [jnp.where(incl, jnp.where(m0, out0[q][c:], out1[q][c:]), 0.0) for q in qs]
    rk_x = [jnp.where(incl, jnp.where(m0, out1[q][c:], out0[q][c:]), 0.0) for q in qs]
    w12 = [_dot(jnp.concatenate([ak_x[q], rk_x[q]], axis=0).astype(BF16),
                jnp.concatenate([jnp.where(m0, 0.0, v_c[q]), jnp.where(m0, v_c[q], 0.0)],
                                axis=0).astype(BF16)) for q in qs]

    zz = [jnp.concatenate([at_c[q], w12[q][:c]], axis=1) for q in qs]
    xp = x_cat
    for step in range(6):
        zz = [zz[q] + _dot(xp[q].astype(BF16), stack_heads(zz[q], m0_256)) for q in qs]
        if step < 5:
            xp = [_dot(xp[q].astype(BF16), stack_heads(xp[q], m0)) for q in qs]
    rbz = [_dot(rb_cat[q].astype(BF16), stack_heads(zz[q], m0_256)) for q in qs]
    q_c = [rt_c[q] + rbz[q][:, 0:LANES] for q in qs]
    z_c = [w12[q][c:] + rbz[q][:, LANES:] for q in qs]
    mn = []
    for q in qs:
        t_src = jnp.concatenate(
            [zz[q], jnp.concatenate([jnp.zeros_like(v_c[q]), v_c[q]], axis=1)], axis=0)
        mn.append(_dot(t_src.T.astype(BF16),
                       jnp.concatenate([bd_c[q], kd_c[q]], axis=0).astype(BF16)))
    m_x = [jnp.where(blockdiag, mn[q][0:LANES], 0.0).astype(BF16) for q in qs]
    n_x = [jnp.where(blockdiag, mn[q][LANES:], 0.0) for q in qs]

    state = st_ref[...]
    ys = []
    for q in qs:
        s_bf = state.astype(BF16)
        ys.append(_nt_dot(q_c[q].astype(BF16), s_bf) + z_c[q])
        state = state * d_end[q * c:q * c + 1, :] + _dot(s_bf, m_x[q]) + n_x[q]

    st_ref[...] = state
    y = jnp.concatenate(ys, axis=0)
    mu = head_sum(y) * (1.0 / n)
    yc = y - mu
    var = head_sum(yc * yc) * (1.0 / n)
    yn = yc * lax.rsqrt(var + RWKV_GN_EPS) * gn_w + gn_b
    o_ref[0] = ((yn + bonus) * g).astype(o_ref.dtype)


def _rwkv(proj3, wd, wa, wg, prm, seq):
    bsz = proj3.shape[0]
    tm = min(1024, seq)
    n_pairs = RWKV_WIDTH // LANES
    blk = lambda off: (lambda b, p, t: (b, t, off // LANES + p))
    kern = functools.partial(_rwkv_kernel, tm=tm)
    return pl.pallas_call(
        kern,
        grid=(bsz, n_pairs, seq // tm),
        in_specs=[pl.BlockSpec((1, tm, LANES), blk(OFF_R)),
                  pl.BlockSpec((1, tm, LANES), blk(OFF_K)),
                  pl.BlockSpec((1, tm, LANES), blk(OFF_V)),
                  pl.BlockSpec((1, tm, LORA_PACK), lambda b, p, t: (b, t, OFF_LORA // LORA_PACK)),
                  pl.BlockSpec((LANES, LANES), lambda b, p, t: (0, p)),
                  pl.BlockSpec((LANES, LANES), lambda b, p, t: (0, p)),
                  pl.BlockSpec((2 * LANES, LANES), lambda b, p, t: (0, p)),
                  pl.BlockSpec((8, LANES), lambda b, p, t: (0, p))],
        out_specs=pl.BlockSpec((1, tm, LANES), lambda b, p, t: (b, t, p)),
        out_shape=jax.ShapeDtypeStruct((bsz, seq, RWKV_WIDTH), BF16),
        scratch_shapes=[pltpu.VMEM((LANES, LANES), F32)],
        compiler_params=_cparams(("arbitrary", "arbitrary", "arbitrary")),
        name="rwkv",
    )(proj3, proj3, proj3, proj3, wd, wa, wg, prm)


def _attn_prep_kernel(q_ref, k_ref, v_ref, cos_ref, sa_ref, sb_ref, gavg_ref, qg_ref, kg_ref,
                      q2_ref, ko_ref, vo_ref):
    width = q_ref.shape[-1]
    reps = width // LANES
    tile = lambda t: jnp.concatenate([t] * reps, axis=1)
    cos_t, sin_a, sin_b = tile(cos_ref[...]), tile(sa_ref[...]), tile(sb_ref[...])
    gavg = gavg_ref[...]
    half = ROPE_DIM // 2

    def norm_rope(x, gain):
        x = x.astype(F32)
        sq = x * x
        hi = sq.astype(BF16)
        lo = (sq - hi.astype(F32)).astype(BF16)
        ms = _dot(hi, gavg) + _dot(lo, gavg)
        y = x * lax.rsqrt(ms + NORM_EPS) * gain
        return (y * cos_t + pltpu.roll(y, width - half, 1) * sin_a + pltpu.roll(y, half, 1) * sin_b)

    qn = norm_rope(q_ref[0], qg_ref[...])
    kn = norm_rope(k_ref[0], kg_ref[...])
    lane = lax.broadcasted_iota(jnp.int32, (1, LANES), 1)
    m0 = lane < DIFF_QK_DIM
    for h in range(DIFF_HEADS):
        sl = slice(h * LANES, (h + 1) * LANES)
        qh = qn[:, sl]
        q2_ref[0, h, 0] = jnp.where(m0, qh, 0.0).astype(BF16)
        q2_ref[0, h, 1] = jnp.where(m0, 0.0, qh).astype(BF16)
        ko_ref[0, h] = kn[:, sl].astype(BF16)
        vo_ref[0, h] = v_ref[0, :, sl]


def _rope_tables(seq):
    half = ROPE_DIM // 2
    inv_freq = ROPE_THETA ** (-jnp.arange(half, dtype=F32) * 2.0 / ROPE_DIM)
    ang = jnp.arange(seq, dtype=jnp.int32).astype(F32)[:, None] * inv_freq[None, :]
    cos, sin = jnp.cos(ang), jnp.sin(ang)
    ones = jnp.ones((seq, DIFF_QK_DIM - ROPE_DIM), F32)
    zeros = jnp.zeros((seq, DIFF_QK_DIM - ROPE_DIM), F32)
    zh = jnp.zeros((seq, half), F32)
    cos64 = jnp.concatenate([cos, cos, ones], axis=1)
    sa64 = jnp.concatenate([-sin, zh, zeros], axis=1)
    sb64 = jnp.concatenate([zh, sin, zeros], axis=1)
    dup = lambda t: jnp.concatenate([t, t], axis=1)
    return dup(cos64), dup(sa64), dup(sb64)


def _attn_prep(proj3, q_gain, k_gain, seq):
    bsz = proj3.shape[0]
    tm = min(512, seq)
    width = DIFF_HEADS * 2 * DIFF_QK_DIM
    cos_t, sin_a, sin_b = _rope_tables(seq)
    gi = jnp.arange(width) // DIFF_QK_DIM
    gavg = jnp.where(gi[:, None] == gi[None, :], 1.0 / DIFF_QK_DIM, 0.0).astype(BF16)
    reps = width // DIFF_QK_DIM
    qg = (jnp.tile(q_gain, reps) * (DIFF_QK_DIM ** -0.5 * math.log2(math.e))).reshape(1, width)
    kg = jnp.tile(k_gain, reps).reshape(1, width)
    hd = (bsz, DIFF_HEADS, seq, LANES)
    tab = pl.BlockSpec((tm, LANES), lambda b, t: (t, 0))
    return pl.pallas_call(
        _attn_prep_kernel,
        grid=(bsz, seq // tm),
        in_specs=[pl.BlockSpec((1, tm, width), lambda b, t: (b, t, OFF_QD // width)),
                  pl.BlockSpec((1, tm, width), lambda b, t: (b, t, OFF_KD // width)),
                  pl.BlockSpec((1, tm, width), lambda b, t: (b, t, OFF_VD // width)),
                  tab, tab, tab,
                  pl.BlockSpec((width, width), lambda b, t: (0, 0)),
                  pl.BlockSpec((1, width), lambda b, t: (0, 0)),
                  pl.BlockSpec((1, width), lambda b, t: (0, 0))],
        out_specs=[pl.BlockSpec((1, DIFF_HEADS, 2, tm, LANES), lambda b, t: (b, 0, 0, t, 0)),
                   pl.BlockSpec((1, DIFF_HEADS, tm, LANES), lambda b, t: (b, 0, t, 0)),
                   pl.BlockSpec((1, DIFF_HEADS, tm, LANES), lambda b, t: (b, 0, t, 0))],
        out_shape=[jax.ShapeDtypeStruct((bsz, DIFF_HEADS, 2, seq, LANES), BF16),
                   jax.ShapeDtypeStruct(hd, BF16), jax.ShapeDtypeStruct(hd, BF16)],
        compiler_params=_cparams(("arbitrary", "arbitrary")),
        name="attn_prep",
    )(proj3, proj3, proj3, cos_t, sin_a, sin_b, gavg, qg, kg)


def _flash_kernel(qi_ref, kj_ref, q_ref, k_ref, v_ref, lam_ref, sg_ref, o_ref, m_ref, acc_ref,
                  *, tq, tk, lambda_init):
    pidx = pl.program_id(2)
    i = qi_ref[pidx]
    j = kj_ref[pidx]

    @pl.when(j == 0)
    def _():
        m_ref[...] = jnp.full(m_ref.shape, -jnp.inf, F32)
        acc_ref[...] = jnp.zeros(acc_ref.shape, F32)

    q = q_ref[0, 0].reshape(2 * tq, LANES)
    s = _nt_dot(q, k_ref[0, 0]).astype(BF16)

    def update(sc):
        m_old = m_ref[...]
        m_new = jnp.maximum(m_old, jnp.max(sc, axis=-1, keepdims=True).astype(F32))
        alpha = jnp.exp2(m_old - m_new)
        m_b = m_new.astype(BF16)
        p = jnp.exp2(sc - jnp.concatenate([m_b] * (tk // LANES), axis=1))
        v = v_ref[0, 0]
        v_ext = jnp.concatenate([v, jnp.ones_like(v)], axis=1)
        acc_ref[...] = jnp.concatenate([alpha, alpha], axis=1) * acc_ref[...] + _dot(p, v_ext)
        m_ref[...] = m_new

    @pl.when(j < i)
    def _():
        update(s)

    @pl.when(j == i)
    def _():
        qpos = lax.broadcasted_iota(jnp.int32, s.shape, 0) % tq
        kpos = lax.broadcasted_iota(jnp.int32, s.shape, 1)
        update(jnp.where(kpos // CHUNK <= qpos // CHUNK, s, -jnp.inf).astype(BF16))
        lam_rows = lam_ref[...]
        lam = (jnp.exp(jnp.sum(lam_rows[0:1] * lam_rows[1:2], axis=-1, keepdims=True))
               - jnp.exp(jnp.sum(lam_rows[2:3] * lam_rows[3:4], axis=-1, keepdims=True))
               + lambda_init)
        acc = acc_ref[...]
        o1 = acc[0:tq, 0:LANES] / acc[0:tq, LANES:]
        o2 = acc[tq:, 0:LANES] / acc[tq:, LANES:]
        o = o1 - lam * o2
        ms = jnp.mean(o * o, axis=-1, keepdims=True)
        o_ref[0] = (o * lax.rsqrt(ms + NORM_EPS) * sg_ref[...] * (1.0 - lambda_init)).astype(o_ref.dtype)


def _flash(q2, kh, vh, lam_rows, subln_gain, seq, lambda_init):
    bsz = q2.shape[0]
    tq = tk = min(1024, seq)
    nq = seq // tq
    pairs = [(i, j) for i in range(nq) for j in range(i + 1)]
    qi = jnp.asarray([p[0] for p in pairs], jnp.int32)
    kj = jnp.asarray([p[1] for p in pairs], jnp.int32)
    kern = functools.partial(_flash_kernel, tq=tq, tk=tk, lambda_init=lambda_init)
    grid_spec = pltpu.PrefetchScalarGridSpec(
        num_scalar_prefetch=2,
        grid=(bsz, DIFF_HEADS, len(pairs)),
        in_specs=[pl.BlockSpec((1, 1, 2, tq, LANES), lambda b, h, p, qi, kj: (b, h, 0, qi[p], 0)),
                  pl.BlockSpec((1, 1, tk, LANES), lambda b, h, p, qi, kj: (b, h, kj[p], 0)),
                  pl.BlockSpec((1, 1, tk, LANES), lambda b, h, p, qi, kj: (b, h, kj[p], 0)),
                  pl.BlockSpec((8, LANES), lambda b, h, p, qi, kj: (0, 0)),
                  pl.BlockSpec((1, LANES), lambda b, h, p, qi, kj: (0, 0))],
        out_specs=pl.BlockSpec((1, tq, LANES), lambda b, h, p, qi, kj: (b, qi[p], h)),
        scratch_shapes=[pltpu.VMEM((2 * tq, LANES), F32), pltpu.VMEM((2 * tq, 2 * LANES), F32)],
    )
    return pl.pallas_call(
        kern,
        grid_spec=grid_spec,
        out_shape=jax.ShapeDtypeStruct((bsz, seq, DIFF_WIDTH), BF16),
        compiler_params=_cparams(("arbitrary", "arbitrary", "arbitrary")),
        name="flash",
    )(qi, kj, q2, kh, vh, lam_rows, subln_gain.reshape(1, LANES))


def _merge_kernel(x_ref, oa_ref, ob_ref, ga_ref, gb_ref, wa_ref, wb_ref, wo_ref, g1_ref,
                  n2_ref, sc2_ref, sh2_ref, wr2_ref, wrh_ref, br_ref,
                  x1_ref, h2_ref, idx_ref, tw_ref, cnt_ref, cnt_scr, *, n_experts):
    ma = _dot(oa_ref[...], wa_ref[...])
    mb = _dot(ob_ref[...], wb_ref[...])
    merged = (jax.nn.sigmoid(ga_ref[...].astype(F32)) * ma
              + jax.nn.sigmoid(gb_ref[...].astype(F32)) * mb)
    x1 = x_ref[...] + g1_ref[0] * _dot(merged.astype(BF16), wo_ref[...])
    x1_ref[...] = x1
    ms = jnp.mean(x1 * x1, axis=-1, keepdims=True)
    h2 = x1 * lax.rsqrt(ms + NORM_EPS) * n2_ref[...] * (1.0 + sc2_ref[0]) + sh2_ref[0]
    h2_ref[...] = h2

    h_hi = h2.astype(BF16)
    h_lo = (h2 - h_hi.astype(F32)).astype(BF16)
    hh = _dot(h_hi, wr2_ref[...])
    logits = hh[:, 0:LANES] + hh[:, LANES:] + _dot(h_lo, wrh_ref[...]) + br_ref[...]
    lane = lax.broadcasted_iota(jnp.int32, logits.shape, 1)
    lane_f = lane.astype(F32)
    lg = jnp.where(lane < n_experts, logits, -jnp.inf)
    vals, idxs = [], []
    for _ in range(TOP_K):
        mx = jnp.max(lg, axis=-1, keepdims=True)
        ix = jnp.min(jnp.where(lg == mx, lane_f, float(LANES)), axis=-1, keepdims=True)
        vals.append(mx)
        idxs.append(ix)
        lg = jnp.where(lane_f == ix, -jnp.inf, lg)
    exps = [jnp.exp(vv - vals[0]) for vv in vals]
    denom = exps[0] + exps[1] + exps[2] + exps[3]
    idx_out = jnp.zeros(logits.shape, F32)
    tw_out = jnp.zeros(logits.shape, F32)
    for kk in range(TOP_K):
        idx_out = jnp.where(lane == kk, idxs[kk], idx_out)
        tw_out = jnp.where(lane == kk, exps[kk] / denom, tw_out)

    @pl.when(pl.program_id(0) == 0)
    def _():
        cnt_scr[...] = jnp.zeros(cnt_scr.shape, F32)

    tm = logits.shape[0]
    hits = jnp.zeros(logits.shape, F32)
    for kk in range(TOP_K):
        hits = hits + jnp.where(lane_f == idxs[kk], 1.0, 0.0)
    ri = lax.broadcasted_iota(jnp.int32, (tm, tm), 0)
    ci = lax.broadcasted_iota(jnp.int32, (tm, tm), 1)
    before = jnp.where(ci < ri, 1.0, 0.0).astype(BF16)
    base = cnt_scr[0:1, :] + _dot(before, hits.astype(BF16))
    for kk in range(TOP_K):
        rank = jnp.sum(jnp.where(lane_f == idxs[kk], base, 0.0), axis=-1, keepdims=True)
        idx_out = jnp.where(lane == TOP_K + kk, rank, idx_out)
    cnt_new = cnt_scr[...] + jnp.sum(hits, axis=0, keepdims=True)
    cnt_scr[...] = cnt_new
    cnt_ref[...] = cnt_new.astype(jnp.int32)
    idx_ref[...] = idx_out.astype(jnp.int32)
    tw_ref[...] = tw_out


def _merge(x2, oa, ob, proj, wa, wb, wo, g1, n2, sc2, sh2, wr, br, seq, n_experts):
    t, d = x2.shape
    tm = min(256, seq)
    tpb = seq // tm
    gw = d
    wr_hi = wr.astype(BF16)
    wr_lo = (wr - wr_hi.astype(F32)).astype(BF16)
    wr2 = jnp.concatenate([wr_hi, wr_lo], axis=1)
    const = lambda shape: pl.BlockSpec(shape, lambda i: tuple(0 for _ in shape),
                                       pipeline_mode=pl.Buffered(1))
    bvec = pl.BlockSpec((1, 1, d), lambda i: (i // tpb, 0, 0))
    kern = functools.partial(_merge_kernel, n_experts=n_experts)
    return pl.pallas_call(
        kern,
        grid=(t // tm,),
        in_specs=[pl.BlockSpec((tm, d), lambda i: (i, 0)),
                  pl.BlockSpec((tm, RWKV_WIDTH), lambda i: (i, 0)),
                  pl.BlockSpec((tm, DIFF_WIDTH), lambda i: (i, 0)),
                  pl.BlockSpec((tm, gw), lambda i: (i, OFF_GA // gw)),
                  pl.BlockSpec((tm, gw), lambda i: (i, OFF_GB // gw)),
                  const(wa.shape), const(wb.shape), const(wo.shape),
                  bvec, const((1, d)), bvec, bvec,
                  const(wr2.shape), const(wr_hi.shape), const(br.shape)],
        out_specs=[pl.BlockSpec((tm, d), lambda i: (i, 0)),
                   pl.BlockSpec((tm, d), lambda i: (i, 0)),
                   pl.BlockSpec((tm, LANES), lambda i: (i, 0)),
                   pl.BlockSpec((tm, LANES), lambda i: (i, 0)),
                   pl.BlockSpec((8, LANES), lambda i: (0, 0))],
        out_shape=[jax.ShapeDtypeStruct((t, d), F32), jax.ShapeDtypeStruct((t, d), F32),
                   jax.ShapeDtypeStruct((t, LANES), jnp.int32),
                   jax.ShapeDtypeStruct((t, LANES), F32),
                   jax.ShapeDtypeStruct((8, LANES), jnp.int32)],
        scratch_shapes=[pltpu.VMEM((8, LANES), F32)],
        compiler_params=_cparams(("arbitrary",)),
        name="merge",
    )(x2, oa, ob, proj, proj, wa, wb, wo, g1, n2, sc2, sh2, wr2, wr_hi, br)


def _row_copy(src_hbm, dst_ref, src_row, dst_row, sem):
    return pltpu.make_async_copy(src_hbm.at[pl.ds(src_row, 1)], dst_ref.at[pl.ds(dst_row, 1)], sem)


def _expert_kernel(be_ref, na_ref, tok0_ref, tokn_ref, h_hbm, wg_ref, bg_ref, wu_ref, bu_ref,
                   wd_ref, bd_ref, ys_ref, xs_scr, xb_scr, sems, *, bm, rows_per_step):
    i = pl.program_id(0)
    f = pl.program_id(1)
    n_act = na_ref[0]
    slot = i % 2

    def wait_slot(s):
        pltpu.make_async_copy(xs_scr.at[s], xs_scr.at[s], sems.at[s]).wait()

    @pl.when(i < n_act)
    def _():
        @pl.when(f == 0)
        def _():
            @pl.when(i == 0)
            def _():
                def issue(r, carry):
                    _row_copy(h_hbm, xs_scr.at[0], tok0_ref[0, 0, r], r, sems.at[0]).start()
                    return carry

                lax.fori_loop(0, bm, issue, 0)

            wait_slot(slot)
            xb_scr[...] = xs_scr[slot].astype(BF16)
            ys_ref[...] = jnp.broadcast_to(bd_ref[0], ys_ref.shape)

        for r in range(rows_per_step):
            _row_copy(h_hbm, xs_scr.at[1 - slot], tokn_ref[0, 0, r], f * rows_per_step + r,
                      sems.at[1 - slot]).start()

        xb = xb_scr[...]
        gt = jnp.minimum(_dot(xb, wg_ref[0].astype(BF16)) + bg_ref[0], SWIGLU_LIMIT)
        up = jnp.clip(_dot(xb, wu_ref[0].astype(BF16)) + bu_ref[0], -SWIGLU_LIMIT, SWIGLU_LIMIT)
        act = (up + 1.0) * gt * jax.nn.sigmoid(SWIGLU_ALPHA * gt)
        ys_ref[...] += _dot(act.astype(BF16), wd_ref[0].astype(BF16))

        @pl.when(jnp.logical_and(f == pl.num_programs(1) - 1, i == n_act - 1))
        def _():
            wait_slot(1 - slot)

    @pl.when(jnp.logical_and(i >= n_act, f == 0))
    def _():
        ys_ref[...] = jnp.zeros(ys_ref.shape, F32)


def _experts(h2, row_token, block_expert, n_active, wg, bg, wu, bu, wd, bd, bm):
    d = h2.shape[1]
    n_exp, _, ff = wg.shape
    tf = 512
    nf = ff // tf
    rows_per_step = bm // nf
    p = row_token.shape[0]
    n_blocks = p // bm

    def row(i, na):
        return jnp.minimum(i, na[0] - 1)

    def next_rows(i, f, be, na):
        return (jnp.minimum(i + 1, n_blocks - 1) * nf + f, 0, 0)

    kern = functools.partial(_expert_kernel, bm=bm, rows_per_step=rows_per_step)
    grid_spec = pltpu.PrefetchScalarGridSpec(
        num_scalar_prefetch=2,
        grid=(n_blocks, nf),
        in_specs=[pl.BlockSpec((1, 1, bm), lambda i, f, be, na: (0, 0, 0), memory_space=pltpu.SMEM),
                  pl.BlockSpec((1, 1, rows_per_step), next_rows, memory_space=pltpu.SMEM),
                  pl.BlockSpec(memory_space=pl.ANY),
                  pl.BlockSpec((1, d, tf), lambda i, f, be, na: (be[row(i, na)], 0, f)),
                  pl.BlockSpec((1, 1, tf), lambda i, f, be, na: (be[row(i, na)], 0, f)),
                  pl.BlockSpec((1, d, tf), lambda i, f, be, na: (be[row(i, na)], 0, f)),
                  pl.BlockSpec((1, 1, tf), lambda i, f, be, na: (be[row(i, na)], 0, f)),
                  pl.BlockSpec((1, tf, d), lambda i, f, be, na: (be[row(i, na)], f, 0)),
                  pl.BlockSpec((1, 1, d), lambda i, f, be, na: (be[row(i, na)], 0, 0))],
        out_specs=pl.BlockSpec((bm, d), lambda i, f, be, na: (i, 0)),
        scratch_shapes=[pltpu.VMEM((2, bm, d), F32), pltpu.VMEM((bm, d), BF16),
                        pltpu.SemaphoreType.DMA((2,))],
    )
    return pl.pallas_call(
        kern,
        grid_spec=grid_spec,
        out_shape=jax.ShapeDtypeStruct((p, d), F32),
        compiler_params=_cparams(("arbitrary", "arbitrary")),
        name="experts",
    )(block_expert, n_active, row_token.reshape(n_blocks, 1, bm),
      row_token.reshape(n_blocks * nf, 1, rows_per_step), h2,
      wg, bg.reshape(n_exp, 1, ff), wu, bu.reshape(n_exp, 1, ff), wd, bd.reshape(n_exp, 1, d))


def _combine_kernel(pos_ref, x1_ref, tw_ref, g2_ref, ys_hbm, o_ref, rows_scr, sem, *, tc):
    def issue(r, carry):
        for kk in range(TOP_K):
            _row_copy(ys_hbm, rows_scr.at[kk], pos_ref[0, 0, r * TOP_K + kk], r, sem).start(
                priority=kk % 2)
        return carry

    lax.fori_loop(0, tc, issue, 0)
    pltpu.make_async_copy(rows_scr, rows_scr, sem).wait()
    tw = tw_ref[...]
    moe = tw[:, 0:1] * rows_scr[0]
    for kk in range(1, TOP_K):
        moe = moe + tw[:, kk:kk + 1] * rows_scr[kk]
    o_ref[...] = x1_ref[...] + g2_ref[0] * moe


def _combine(x1, top_w, g2, ys, pos, seq):
    t, d = x1.shape
    tc = min(128, seq)
    tpb = seq // tc
    n_steps = t // tc
    kern = functools.partial(_combine_kernel, tc=tc)
    return pl.pallas_call(
        kern,
        grid=(n_steps,),
        in_specs=[pl.BlockSpec((1, 1, tc * TOP_K), lambda i: (i, 0, 0), memory_space=pltpu.SMEM),
                  pl.BlockSpec((tc, d), lambda i: (i, 0)),
                  pl.BlockSpec((tc, LANES), lambda i: (i, 0)),
                  pl.BlockSpec((1, 1, d), lambda i: (i // tpb, 0, 0)),
                  pl.BlockSpec(memory_space=pl.ANY)],
        out_specs=pl.BlockSpec((tc, d), lambda i: (i, 0)),
        out_shape=jax.ShapeDtypeStruct((t, d), F32),
        scratch_shapes=[pltpu.VMEM((TOP_K, tc, d), F32), pltpu.SemaphoreType.DMA(())],
        compiler_params=_cparams(("arbitrary",)),
        name="combine",
    )(pos.reshape(n_steps, 1, tc * TOP_K), x1, top_w, g2, ys)


def _route(top_idx, rank, counts, bm):
    t = top_idx.shape[0]
    n_experts = counts.shape[0]
    m = t * TOP_K
    n_blocks = (m + n_experts * (bm - 1) + bm - 1) // bm
    flat_e = top_idx.reshape(m)
    rank = rank.reshape(m)
    padded = (counts + bm - 1) // bm * bm
    pad_end = jnp.cumsum(padded)
    pad_start = pad_end - padded
    dest = (pad_start[flat_e] + rank).astype(jnp.int32)
    row_token = jnp.zeros((n_blocks * bm,), jnp.int32).at[dest].set(
        jnp.arange(m, dtype=jnp.int32) // TOP_K, unique_indices=True)
    block_start = jnp.arange(n_blocks, dtype=jnp.int32) * bm
    block_expert = jnp.minimum(jnp.searchsorted(pad_end, block_start, side='right'),
                               n_experts - 1).astype(jnp.int32)
    n_active = (pad_end[-1] // bm).astype(jnp.int32).reshape(1)
    return row_token, dest, block_expert, n_active


def _moe(x1, h2, top_idx, top_w, counts, g2, wg, bg, wu, bu, wd, bd, seq):
    n_experts = wg.shape[0]
    bm = 768
    row_token, dest, block_expert, n_active = _route(
        top_idx[:, :TOP_K], top_idx[:, TOP_K:2 * TOP_K], counts[0, :n_experts], bm)
    ys = _experts(h2, row_token, block_expert, n_active, wg, bg, wu, bu, wd, bd, bm)
    return _combine(x1, top_w, g2, ys, dest, seq)


def _layer(x, c, l, w_ada, b_ada, norm1_gain, norm2_gain, w_in, shift_mu, w0, w_decay_up, a0,
           w_aaa_up, w_gate_up, k_k, k_a, r_k, gn_w, gn_b, q_gain, k_gain, lam_q1, lam_k1,
           lam_q2, lam_k2, subln_gain, w_branch_a, w_branch_b, w_out, w_router, b_router,
           w_exp_gate, b_exp_gate, w_exp_up, b_exp_up, w_exp_down, b_exp_down):
    bsz, seq, d = x.shape
    t = bsz * seq
    mod = _ada(c, w_ada, b_ada)
    sh1, sc1, g1, sh2, sc2, g2 = (m.reshape(bsz, 1, d) for m in jnp.split(mod, 6, axis=-1))

    decay_rank, aaa_rank, gate_rank = w_decay_up.shape[0], w_aaa_up.shape[0], w_gate_up.shape[0]
    w_packed, mu_packed = _pack_in_weights(w_in, shift_mu, decay_rank, aaa_rank, gate_rank)
    x2 = x.reshape(t, d)
    proj = _inproj(x2, norm1_gain.reshape(1, d), sc1, sh1, w_packed, mu_packed, seq)
    proj3 = proj.reshape(bsz, seq, PACKED_WIDTH)

    padr = lambda w: jnp.pad(w, ((0, LANES - w.shape[0]), (0, 0))).astype(BF16)
    prm = jnp.stack([w0, a0, k_k, k_a, r_k, gn_w, gn_b, jnp.zeros_like(w0)], axis=0)
    o_a = _rwkv(proj3, padr(w_decay_up), padr(w_aaa_up), w_gate_up.astype(BF16), prm, seq)

    lambda_init = 0.8 - 0.6 * math.exp(-0.3 * l)
    q2, kh, vh = _attn_prep(proj3, q_gain, k_gain, seq)
    lam_rows = jnp.pad(jnp.stack([lam_q1, lam_k1, lam_q2, lam_k2], axis=0),
                       ((0, 4), (0, LANES - DIFF_QK_DIM)))
    o_b = _flash(q2, kh, vh, lam_rows, subln_gain, seq, lambda_init)

    n_experts = w_router.shape[1]
    wr = jnp.pad(w_router, ((0, 0), (0, LANES - n_experts)))
    br = jnp.pad(b_router, (0, LANES - n_experts)).reshape(1, LANES)
    x1, h2, top_idx, top_w, counts = _merge(
        x2, o_a.reshape(t, RWKV_WIDTH), o_b.reshape(t, DIFF_WIDTH), proj,
        w_branch_a.astype(BF16), w_branch_b.astype(BF16), w_out.astype(BF16),
        g1, norm2_gain.reshape(1, d), sc2, sh2, wr, br, seq, n_experts)

    out = _moe(x1, h2, top_idx, top_w, counts, g2, w_exp_gate, b_exp_gate, w_exp_up, b_exp_up,
               w_exp_down, b_exp_down, seq)
    return out.reshape(bsz, seq, d)


def kernel(x, c, w_ada, b_ada, norm1_gain, norm2_gain, w_in, shift_mu, w0, w_decay_up, a0, w_aaa_up, w_gate_up, k_k, k_a, r_k, gn_w, gn_b, q_gain, k_gain, lam_q1, lam_k1, lam_q2, lam_k2, subln_gain, w_branch_a, w_branch_b, w_out, w_router, b_router, w_exp_gate, b_exp_gate, w_exp_up, b_exp_up, w_exp_down, b_exp_down):
    per_layer = (w_ada, b_ada, norm1_gain, norm2_gain, w_in, shift_mu, w0, w_decay_up, a0,
                 w_aaa_up, w_gate_up, k_k, k_a, r_k, gn_w, gn_b, q_gain, k_gain, lam_q1, lam_k1,
                 lam_q2, lam_k2, subln_gain, w_branch_a, w_branch_b, w_out, w_router, b_router,
                 w_exp_gate, b_exp_gate, w_exp_up, b_exp_up, w_exp_down, b_exp_down)
    for l in range(w_in.shape[0]):
        x = _layer(x, c, l, *(p[l] for p in per_layer))
    return x
```

```python
import functools
import math

import jax
import jax.numpy as jnp
from jax import lax
from jax.experimental import pallas as pl
from jax.experimental.pallas import tpu as pltpu

F32 = jnp.float32
BF16 = jnp.bfloat16

NORM_EPS = 1e-6
CHUNK = 64

RWKV_HEADS = 16
RWKV_HEAD_DIM = 64
RWKV_WIDTH = RWKV_HEADS * RWKV_HEAD_DIM
RWKV_GN_EPS = 64e-5

DIFF_HEADS = 8
DIFF_QK_DIM = 64
DIFF_V_DIM = 128
DIFF_WIDTH = DIFF_HEADS * DIFF_V_DIM
ROPE_THETA = 500000.0
ROPE_DIM = DIFF_QK_DIM // 4

TOP_K = 4
SWIGLU_LIMIT = 7.0
SWIGLU_ALPHA = 1.702

LANES = 128
LORA_PACK = 512
VMEM_LIMIT = 56 * 1024 * 1024

OFF_R, OFF_K, OFF_V = 0, 1024, 2048
OFF_QD, OFF_KD, OFF_VD = 3072, 4096, 5120
OFF_GA, OFF_GB = 6144, 8192
OFF_LORA = 10240
PACKED_WIDTH = OFF_LORA + LORA_PACK


def _cparams(sem):
    return pltpu.CompilerParams(dimension_semantics=sem, vmem_limit_bytes=VMEM_LIMIT)


def _nt_dot(a, b):
    return lax.dot_general(a, b, (((1,), (1,)), ((), ())), preferred_element_type=F32)


def _dot(a, b):
    return jnp.dot(a, b, preferred_element_type=F32)


def _ada_kernel(c_ref, w_ref, b_ref, o_ref):
    c = c_ref[...]
    s = c * jax.nn.sigmoid(c)
    o_ref[...] = _dot(s.astype(BF16), w_ref[...].astype(BF16)) + b_ref[...]


def _ada(c, w, b):
    bsz, d = c.shape
    n = w.shape[1]
    rows = 8
    tn = 1024
    cp = jnp.pad(c, ((0, rows - bsz), (0, 0)))
    out = pl.pallas_call(
        _ada_kernel,
        grid=(n // tn,),
        in_specs=[pl.BlockSpec((rows, d), lambda j: (0, 0)),
                  pl.BlockSpec((d, tn), lambda j: (0, j)),
                  pl.BlockSpec((1, tn), lambda j: (0, j))],
        out_specs=pl.BlockSpec((rows, tn), lambda j: (0, j)),
        out_shape=jax.ShapeDtypeStruct((rows, n), F32),
        compiler_params=_cparams(("arbitrary",)),
        name="ada",
    )(cp, w, b.reshape(1, n))
    return out[:bsz]


def _inproj_kernel(x_ref, g_ref, sc_ref, sh_ref, w_ref, mu_ref, o_ref, h_scr, carry_scr,
                   *, tiles_per_batch, n_rkv_tiles, lora_tile):
    i = pl.program_id(0)
    j = pl.program_id(1)

    @pl.when(j == 0)
    def _():
        x = x_ref[...]
        ms = jnp.mean(x * x, axis=-1, keepdims=True)
        y = x * lax.rsqrt(ms + NORM_EPS) * g_ref[...]
        h_scr[...] = (y * (1.0 + sc_ref[0]) + sh_ref[0]).astype(BF16)

    p = _dot(h_scr[...], w_ref[...])
    shifted = jnp.logical_or(j < n_rkv_tiles, j == lora_tile)

    @pl.when(shifted)
    def _():
        slot = jnp.where(j == lora_tile, n_rkv_tiles, j)

        @pl.when(i % tiles_per_batch == 0)
        def _():
            carry_scr[slot] = jnp.zeros(carry_scr.shape[1:], F32)

        prev_last = carry_scr[slot]
        tm = p.shape[0]
        carry_scr[slot] = p[tm - 1:tm, :]
        row = lax.broadcasted_iota(jnp.int32, p.shape, 0)
        prev = jnp.where(row == 0, prev_last, pltpu.roll(p, 1, 0))
        o_ref[...] = (p + (prev - p) * mu_ref[...]).astype(o_ref.dtype)

    @pl.when(jnp.logical_not(shifted))
    def _():
        o_ref[...] = p.astype(o_ref.dtype)


def _inproj(x2, gain, sc, sh, w_packed, mu_packed, seq):
    t, d = x2.shape
    tm = min(1024, seq)
    tn = 1536
    assert PACKED_WIDTH % tn == 0 and OFF_QD % tn == 0 and PACKED_WIDTH - tn <= OFF_LORA
    n_col = PACKED_WIDTH // tn
    n_rkv_tiles = OFF_QD // tn
    lora_tile = n_col - 1
    tiles_per_batch = seq // tm

    kern = functools.partial(_inproj_kernel, tiles_per_batch=tiles_per_batch,
                             n_rkv_tiles=n_rkv_tiles, lora_tile=lora_tile)
    return pl.pallas_call(
        kern,
        grid=(t // tm, n_col),
        in_specs=[pl.BlockSpec((tm, d), lambda i, j: (i, 0)),
                  pl.BlockSpec((1, d), lambda i, j: (0, 0)),
                  pl.BlockSpec((1, 1, d), lambda i, j: (i // tiles_per_batch, 0, 0)),
                  pl.BlockSpec((1, 1, d), lambda i, j: (i // tiles_per_batch, 0, 0)),
                  pl.BlockSpec((d, tn), lambda i, j: (0, j)),
                  pl.BlockSpec((1, tn), lambda i, j: (0, j))],
        out_specs=pl.BlockSpec((tm, tn), lambda i, j: (i, j)),
        out_shape=jax.ShapeDtypeStruct((t, PACKED_WIDTH), BF16),
        scratch_shapes=[pltpu.VMEM((tm, d), BF16),
                        pltpu.VMEM((n_rkv_tiles + 1, 1, tn), F32)],
        compiler_params=_cparams(("arbitrary", "arbitrary")),
        name="inproj",
    )(x2, gain, sc, sh, w_packed, mu_packed)


def _pack_in_weights(w_in, shift_mu, decay_rank, aaa_rank, gate_rank):
    assert decay_rank <= LANES and aaa_rank <= LANES and gate_rank == 2 * LANES
    rw = RWKV_WIDTH
    o = 3 * rw
    w_r, w_k, w_v = w_in[:, 0:rw], w_in[:, rw:2 * rw], w_in[:, 2 * rw:3 * rw]
    w_dl = w_in[:, o:o + decay_rank]
    w_al = w_in[:, o + decay_rank:o + decay_rank + aaa_rank]
    w_gl = w_in[:, o + decay_rank + aaa_rank:o + decay_rank + aaa_rank + gate_rank]
    rest = w_in[:, o + decay_rank + aaa_rank + gate_rank:]

    def padc(a, width):
        return jnp.pad(a, ((0, 0), (0, width - a.shape[1])))

    packed = jnp.concatenate(
        [w_r, w_k, w_v, rest, padc(w_dl, LANES), padc(w_al, LANES), w_gl], axis=1).astype(BF16)
    assert packed.shape[1] == PACKED_WIDTH
    mu = shift_mu.reshape(1, -1)
    mu_packed = jnp.concatenate(
        [mu[:, :o], jnp.zeros((1, OFF_LORA - o), F32), padc(mu[:, o:o + decay_rank], LANES),
         padc(mu[:, o + decay_rank:o + decay_rank + aaa_rank], LANES),
         mu[:, o + decay_rank + aaa_rank:]], axis=1)
    return packed, mu_packed


def _split3_bf16(x):
    hi = x.astype(BF16)
    r1 = x - hi.astype(F32)
    mid = r1.astype(BF16)
    lo = (r1 - mid.astype(F32)).astype(BF16)
    return hi, mid, lo


def _rwkv_kernel(r_ref, k_ref, v_ref, lora_ref, wd_ref, wa_ref, wg_ref, prm_ref, o_ref, st_ref,
                 *, tm):
    c = CHUNK
    n = RWKV_HEAD_DIM
    n_chunks = tm // c

    @pl.when(pl.program_id(2) == 0)
    def _():
        st_ref[...] = jnp.zeros(st_ref.shape, F32)

    lane = lax.broadcasted_iota(jnp.int32, (1, LANES), 1)
    m0 = lane < n
    m0_256 = jnp.concatenate([m0, m0], axis=1)

    def head_sum(x):
        s0 = jnp.sum(jnp.where(m0, x, 0.0), axis=-1, keepdims=True)
        s1 = jnp.sum(jnp.where(m0, 0.0, x), axis=-1, keepdims=True)
        return jnp.where(m0, s0, s1)

    prm = prm_ref[...]
    w0, a0, k_k, k_a, r_k, gn_w, gn_b = (prm[i:i + 1] for i in range(7))

    r = r_ref[0].astype(F32)
    k = k_ref[0].astype(F32)
    v = v_ref[0].astype(F32)
    lora = lora_ref[0]
    d_code = lora[:, 0:LANES].astype(F32)
    a_code = lora[:, LANES:2 * LANES]
    g_code = lora[:, 2 * LANES:4 * LANES].astype(F32)

    wl = w0 + _dot(jnp.tanh(d_code).astype(BF16), wd_ref[...])
    z = -wl
    softplus = jnp.maximum(z, 0.0) + jnp.log(1.0 + jnp.exp(-jnp.abs(z)))
    lw = -jnp.exp(-softplus - 0.5)
    a = jax.nn.sigmoid(a0 + _dot(a_code, wa_ref[...]))
    g = _dot(jax.nn.sigmoid(g_code).astype(BF16), wg_ref[...])

    kk = k * k_k
    kk = kk / jnp.maximum(jnp.sqrt(head_sum(kk * kk)), 1e-12)
    k2 = k * (1.0 + (a - 1.0) * k_a)
    a_vec = -kk
    b_vec = kk * a
    bonus = head_sum(r * k2 * r_k) * v

    tb = min(tm, 2 * LANES)
    ri = lax.broadcasted_iota(jnp.int32, (tb, tb), 0)
    ci = lax.broadcasted_iota(jnp.int32, (tb, tb), 1)
    tri = jnp.where(jnp.logical_and(ci <= ri, ci // c == ri // c), 1.0, 0.0).astype(BF16)
    lw3 = jnp.concatenate(_split3_bf16(lw), axis=1)
    cum3 = jnp.concatenate([_dot(tri, lw3[u * tb:(u + 1) * tb]) for u in range(tm // tb)], axis=0)
    cum = cum3[:, 0:LANES] + cum3[:, LANES:2 * LANES] + cum3[:, 2 * LANES:3 * LANES]
    cum_end = jnp.concatenate(
        [jnp.broadcast_to(cum[(q + 1) * c - 1:(q + 1) * c, :], (c, LANES)) for q in range(n_chunks)],
        axis=0)

    e_inc = jnp.exp(cum)
    e_exc = jnp.exp(cum - lw)
    e_neg = jnp.exp(-cum)
    e_end = jnp.exp(cum_end - cum)
    a_t = a_vec * e_exc
    r_t = r * e_inc
    k_h = k2 * e_neg
    b_h = b_vec * e_neg
    k_d = k2 * e_end
    b_d = b_vec * e_end
    d_end = jnp.exp(cum_end)

    row64 = lax.broadcasted_iota(jnp.int32, (c, LANES), 0)
    col64 = lax.broadcasted_iota(jnp.int32, (c, LANES), 1) % n
    strict = row64 > col64
    incl = row64 >= col64
    rb = lax.broadcasted_iota(jnp.int32, (LANES, LANES), 0) < n
    cb = lax.broadcasted_iota(jnp.int32, (LANES, LANES), 1) < n
    blockdiag = rb == cb

    def stack_heads(zz, mask):
        return jnp.concatenate([jnp.where(mask, zz, 0.0), jnp.where(mask, 0.0, zz)],
                               axis=0).astype(BF16)

    qs = range(n_chunks)
    ch = lambda t: [t[q * c:(q + 1) * c] for q in qs]
    at_c, rt_c, kh_c, bh_c, kd_c, bd_c, v_c = (ch(t) for t in (a_t, r_t, k_h, b_h, k_d, b_d, v))
    out0 = [_nt_dot(jnp.concatenate([jnp.where(m0, at_c[q], 0.0), jnp.where(m0, rt_c[q], 0.0)],
                                    axis=0).astype(BF16),
                    jnp.concatenate([bh_c[q], kh_c[q]], axis=0).astype(BF16)) for q in qs]
    out1 = [_nt_dot(jnp.concatenate([jnp.where(m0, 0.0, at_c[q]), jnp.where(m0, 0.0, rt_c[q])],
                                    axis=0).astype(BF16),
                    jnp.concatenate([kh_c[q], bh_c[q]], axis=0).astype(BF16)) for q in qs]
    x_cat = [jnp.where(strict, jnp.where(m0, out0[q][:c], out1[q][:c]), 0.0) for q in qs]
    ak_x = [jnp.where(strict, jnp.where(m0, out1[q][:c], out0[q][:c]), 0.0) for q in qs]
    rb_cat = [jnp.where(incl, jnp.where(m0, out0[q][c:], out1[q][c:]), 0.0) for q in qs]
    rk_x = [jnp.where(incl, jnp.where(m0, out1[q][c:], out0[q][c:]), 0.0) for q in qs]
    w12 = [_dot(jnp.concatenate([ak_x[q], rk_x[q]], axis=0).astype(BF16),
                jnp.concatenate([jnp.where(m0, 0.0, v_c[q]), jnp.where(m0, v_c[q], 0.0)],
                                axis=0).astype(BF16)) for q in qs]

    zz = [jnp.concatenate([at_c[q], w12[q][:c]], axis=1) for q in qs]
    xp = x_cat
    for step in range(6):
        zz = [zz[q] + _dot(xp[q].astype(BF16), stack_heads(zz[q], m0_256)) for q in qs]
        if step < 5:
            xp = [_dot(xp[q].astype(BF16), stack_heads(xp[q], m0)) for q in qs]
    rbz = [_dot(rb_cat[q].astype(BF16), stack_heads(zz[q], m0_256)) for q in qs]
    q_c = [rt_c[q] + rbz[q][:, 0:LANES] for q in qs]
    z_c = [w12[q][c:] + rbz[q][:, LANES:] for q in qs]
    mn = []
    for q in qs:
        t_src = jnp.concatenate(
            [zz[q], jnp.concatenate([jnp.zeros_like(v_c[q]), v_c[q]], axis=1)], axis=0)
        mn.append(_dot(t_src.T.astype(BF16),
                       jnp.concatenate([bd_c[q], kd_c[q]], axis=0).astype(BF16)))
    m_x = [jnp.where(blockdiag, mn[q][0:LANES], 0.0).astype(BF16) for q in qs]
    n_x = [jnp.where(blockdiag, mn[q][LANES:], 0.0) for q in qs]

    state = st_ref[...]
    ys = []
    for q in qs:
        s_bf = state.astype(BF16)
        ys.append(_nt_dot(q_c[q].astype(BF16), s_bf) + z_c[q])
        state = state * d_end[q * c:q * c + 1, :] + _dot(s_bf, m_x[q]) + n_x[q]

    st_ref[...] = state
    y = jnp.concatenate(ys, axis=0)
    mu = head_sum(y) * (1.0 / n)
    yc = y - mu
    var = head_sum(yc * yc) * (1.0 / n)
    yn = yc * lax.rsqrt(var + RWKV_GN_EPS) * gn_w + gn_b
    o_ref[0] = ((yn + bonus) * g).astype(o_ref.dtype)


def _rwkv(proj3, wd, wa, wg, prm, seq):
    bsz = proj3.shape[0]
    tm = min(1024, seq)
    n_pairs = RWKV_WIDTH // LANES
    blk = lambda off: (lambda b, p, t: (b, t, off // LANES + p))
    kern = functools.partial(_rwkv_kernel, tm=tm)
    return pl.pallas_call(
        kern,
        grid=(bsz, n_pairs, seq // tm),
        in_specs=[pl.BlockSpec((1, tm, LANES), blk(OFF_R)),
                  pl.BlockSpec((1, tm, LANES), blk(OFF_K)),
                  pl.BlockSpec((1, tm, LANES), blk(OFF_V)),
                  pl.BlockSpec((1, tm, LORA_PACK), lambda b, p, t: (b, t, OFF_LORA // LORA_PACK)),
                  pl.BlockSpec((LANES, LANES), lambda b, p, t: (0, p)),
                  pl.BlockSpec((LANES, LANES), lambda b, p, t: (0, p)),
                  pl.BlockSpec((2 * LANES, LANES), lambda b, p, t: (0, p)),
                  pl.BlockSpec((8, LANES), lambda b, p, t: (0, p))],
        out_specs=pl.BlockSpec((1, tm, LANES), lambda b, p, t: (b, t, p)),
        out_shape=jax.ShapeDtypeStruct((bsz, seq, RWKV_WIDTH), BF16),
        scratch_shapes=[pltpu.VMEM((LANES, LANES), F32)],
        compiler_params=_cparams(("arbitrary", "arbitrary", "arbitrary")),
        name="rwkv",
    )(proj3, proj3, proj3, proj3, wd, wa, wg, prm)


def _attn_prep_kernel(q_ref, k_ref, v_ref, cos_ref, sa_ref, sb_ref, gavg_ref, qg_ref, kg_ref,
                      q2_ref, ko_ref, vo_ref):
    width = q_ref.shape[-1]
    reps = width // LANES
    tile = lambda t: jnp.concatenate([t] * reps, axis=1)
    cos_t, sin_a, sin_b = tile(cos_ref[...]), tile(sa_ref[...]), tile(sb_ref[...])
    gavg = gavg_ref[...]
    half = ROPE_DIM // 2

    def norm_rope(x, gain):
        x = x.astype(F32)
        sq = x * x
        hi = sq.astype(BF16)
        lo = (sq - hi.astype(F32)).astype(BF16)
        gw = gavg.shape[0]
        ms = jnp.concatenate(
            [_dot(hi[:, u * gw:(u + 1) * gw], gavg) + _dot(lo[:, u * gw:(u + 1) * gw], gavg)
             for u in range(width // gw)], axis=1)
        y = x * lax.rsqrt(ms + NORM_EPS) * gain
        return (y * cos_t + pltpu.roll(y, width - half, 1) * sin_a + pltpu.roll(y, half, 1) * sin_b)

    qn = norm_rope(q_ref[0], qg_ref[...])
    kn = norm_rope(k_ref[0], kg_ref[...])
    lane = lax.broadcasted_iota(jnp.int32, (1, LANES), 1)
    m0 = lane < DIFF_QK_DIM
    for h in range(DIFF_HEADS):
        sl = slice(h * LANES, (h + 1) * LANES)
        qh = qn[:, sl]
        q2_ref[0, h, 0] = jnp.where(m0, qh, 0.0).astype(BF16)
        q2_ref[0, h, 1] = jnp.where(m0, 0.0, qh).astype(BF16)
        ko_ref[0, h] = kn[:, sl].astype(BF16)
        vo_ref[0, h] = v_ref[0, :, sl]


def _rope_tables(seq):
    half = ROPE_DIM // 2
    inv_freq = ROPE_THETA ** (-jnp.arange(half, dtype=F32) * 2.0 / ROPE_DIM)
    ang = jnp.arange(seq, dtype=jnp.int32).astype(F32)[:, None] * inv_freq[None, :]
    cos, sin = jnp.cos(ang), jnp.sin(ang)
    ones = jnp.ones((seq, DIFF_QK_DIM - ROPE_DIM), F32)
    zeros = jnp.zeros((seq, DIFF_QK_DIM - ROPE_DIM), F32)
    zh = jnp.zeros((seq, half), F32)
    cos64 = jnp.concatenate([cos, cos, ones], axis=1)
    sa64 = jnp.concatenate([-sin, zh, zeros], axis=1)
    sb64 = jnp.concatenate([zh, sin, zeros], axis=1)
    dup = lambda t: jnp.concatenate([t, t], axis=1)
    return dup(cos64), dup(sa64), dup(sb64)


def _attn_prep(proj3, q_gain, k_gain, seq):
    bsz = proj3.shape[0]
    tm = min(512, seq)
    width = DIFF_HEADS * 2 * DIFF_QK_DIM
    cos_t, sin_a, sin_b = _rope_tables(seq)
    gi = jnp.arange(2 * LANES) // DIFF_QK_DIM
    gavg = jnp.where(gi[:, None] == gi[None, :], 1.0 / DIFF_QK_DIM, 0.0).astype(BF16)
    reps = width // DIFF_QK_DIM
    qg = (jnp.tile(q_gain, reps) * (DIFF_QK_DIM ** -0.5 * math.log2(math.e))).reshape(1, width)
    kg = jnp.tile(k_gain, reps).reshape(1, width)
    hd = (bsz, DIFF_HEADS, seq, LANES)
    tab = pl.BlockSpec((tm, LANES), lambda b, t: (t, 0))
    return pl.pallas_call(
        _attn_prep_kernel,
        grid=(bsz, seq // tm),
        in_specs=[pl.BlockSpec((1, tm, width), lambda b, t: (b, t, OFF_QD // width)),
                  pl.BlockSpec((1, tm, width), lambda b, t: (b, t, OFF_KD // width)),
                  pl.BlockSpec((1, tm, width), lambda b, t: (b, t, OFF_VD // width)),
                  tab, tab, tab,
                  pl.BlockSpec((2 * LANES, 2 * LANES), lambda b, t: (0, 0)),
                  pl.BlockSpec((1, width), lambda b, t: (0, 0)),
                  pl.BlockSpec((1, width), lambda b, t: (0, 0))],
        out_specs=[pl.BlockSpec((1, DIFF_HEADS, 2, tm, LANES), lambda b, t: (b, 0, 0, t, 0)),
                   pl.BlockSpec((1, DIFF_HEADS, tm, LANES), lambda b, t: (b, 0, t, 0)),
                   pl.BlockSpec((1, DIFF_HEADS, tm, LANES), lambda b, t: (b, 0, t, 0))],
        out_shape=[jax.ShapeDtypeStruct((bsz, DIFF_HEADS, 2, seq, LANES), BF16),
                   jax.ShapeDtypeStruct(hd, BF16), jax.ShapeDtypeStruct(hd, BF16)],
        compiler_params=_cparams(("arbitrary", "arbitrary")),
        name="attn_prep",
    )(proj3, proj3, proj3, cos_t, sin_a, sin_b, gavg, qg, kg)


def _flash_kernel(qi_ref, kj_ref, q_ref, k_ref, v_ref, lam_ref, sg_ref, o_ref, m_ref, acc_ref,
                  *, tq, tk, lambda_init):
    pidx = pl.program_id(2)
    i = qi_ref[pidx]
    j = kj_ref[pidx]

    @pl.when(j == 0)
    def _():
        m_ref[...] = jnp.full(m_ref.shape, -jnp.inf, F32)
        acc_ref[...] = jnp.zeros(acc_ref.shape, F32)

    q = q_ref[0, 0].reshape(2 * tq, LANES)
    s = _nt_dot(q, k_ref[0, 0]).astype(BF16)

    def update(sc):
        m_old = m_ref[...]
        m_new = jnp.maximum(m_old, jnp.max(sc, axis=-1, keepdims=True).astype(F32))
        alpha = jnp.exp2(m_old - m_new)
        m_b = m_new.astype(BF16)
        p = jnp.exp2(sc - jnp.concatenate([m_b] * (tk // LANES), axis=1))
        v = v_ref[0, 0]
        v_ext = jnp.concatenate([v, jnp.ones_like(v)], axis=1)
        acc_ref[...] = jnp.concatenate([alpha, alpha], axis=1) * acc_ref[...] + _dot(p, v_ext)
        m_ref[...] = m_new

    @pl.when(j < i)
    def _():
        update(s)

    @pl.when(j == i)
    def _():
        qpos = lax.broadcasted_iota(jnp.int32, s.shape, 0) % tq
        kpos = lax.broadcasted_iota(jnp.int32, s.shape, 1)
        update(jnp.where(kpos // CHUNK <= qpos // CHUNK, s, -jnp.inf).astype(BF16))
        lam_rows = lam_ref[...]
        lam = (jnp.exp(jnp.sum(lam_rows[0:1] * lam_rows[1:2], axis=-1, keepdims=True))
               - jnp.exp(jnp.sum(lam_rows[2:3] * lam_rows[3:4], axis=-1, keepdims=True))
               + lambda_init)
        acc = acc_ref[...]
        o1 = acc[0:tq, 0:LANES] / acc[0:tq, LANES:]
        o2 = acc[tq:, 0:LANES] / acc[tq:, LANES:]
        o = o1 - lam * o2
        ms = jnp.mean(o * o, axis=-1, keepdims=True)
        o_ref[0] = (o * lax.rsqrt(ms + NORM_EPS) * sg_ref[...] * (1.0 - lambda_init)).astype(o_ref.dtype)


def _flash(q2, kh, vh, lam_rows, subln_gain, seq, lambda_init):
    bsz = q2.shape[0]
    tq = tk = min(1024, seq)
    nq = seq // tq
    pairs = [(i, j) for i in range(nq) for j in range(i + 1)]
    qi = jnp.asarray([p[0] for p in pairs], jnp.int32)
    kj = jnp.asarray([p[1] for p in pairs], jnp.int32)
    kern = functools.partial(_flash_kernel, tq=tq, tk=tk, lambda_init=lambda_init)
    grid_spec = pltpu.PrefetchScalarGridSpec(
        num_scalar_prefetch=2,
        grid=(bsz, DIFF_HEADS, len(pairs)),
        in_specs=[pl.BlockSpec((1, 1, 2, tq, LANES), lambda b, h, p, qi, kj: (b, h, 0, qi[p], 0)),
                  pl.BlockSpec((1, 1, tk, LANES), lambda b, h, p, qi, kj: (b, h, kj[p], 0)),
                  pl.BlockSpec((1, 1, tk, LANES), lambda b, h, p, qi, kj: (b, h, kj[p], 0)),
                  pl.BlockSpec((8, LANES), lambda b, h, p, qi, kj: (0, 0)),
                  pl.BlockSpec((1, LANES), lambda b, h, p, qi, kj: (0, 0))],
        out_specs=pl.BlockSpec((1, tq, LANES), lambda b, h, p, qi, kj: (b, qi[p], h)),
        scratch_shapes=[pltpu.VMEM((2 * tq, LANES), F32), pltpu.VMEM((2 * tq, 2 * LANES), F32)],
    )
    return pl.pallas_call(
        kern,
        grid_spec=grid_spec,
        out_shape=jax.ShapeDtypeStruct((bsz, seq, DIFF_WIDTH), BF16),
        compiler_params=_cparams(("arbitrary", "arbitrary", "arbitrary")),
        name="flash",
    )(qi, kj, q2, kh, vh, lam_rows, subln_gain.reshape(1, LANES))


def _merge_kernel(x_ref, oa_ref, ob_ref, ga_ref, gb_ref, wa_ref, wb_ref, wo_ref, g1_ref,
                  n2_ref, sc2_ref, sh2_ref, wr2_ref, wrh_ref, br_ref,
                  x1_ref, h2_ref, idx_ref, tw_ref, cnt_ref, cnt_scr, *, n_experts):
    ma = _dot(oa_ref[...], wa_ref[...])
    mb = _dot(ob_ref[...], wb_ref[...])
    merged = (jax.nn.sigmoid(ga_ref[...].astype(F32)) * ma
              + jax.nn.sigmoid(gb_ref[...].astype(F32)) * mb)
    x1 = x_ref[...] + g1_ref[0] * _dot(merged.astype(BF16), wo_ref[...])
    x1_ref[...] = x1
    ms = jnp.mean(x1 * x1, axis=-1, keepdims=True)
    h2 = x1 * lax.rsqrt(ms + NORM_EPS) * n2_ref[...] * (1.0 + sc2_ref[0]) + sh2_ref[0]
    h2_ref[...] = h2

    h_hi = h2.astype(BF16)
    h_lo = (h2 - h_hi.astype(F32)).astype(BF16)
    hh = _dot(h_hi, wr2_ref[...])
    logits = hh[:, 0:LANES] + hh[:, LANES:] + _dot(h_lo, wrh_ref[...]) + br_ref[...]
    lane = lax.broadcasted_iota(jnp.int32, logits.shape, 1)
    lane_f = lane.astype(F32)
    lg = jnp.where(lane < n_experts, logits, -jnp.inf)
    vals, idxs = [], []
    for _ in range(TOP_K):
        mx = jnp.max(lg, axis=-1, keepdims=True)
        ix = jnp.min(jnp.where(lg == mx, lane_f, float(LANES)), axis=-1, keepdims=True)
        vals.append(mx)
        idxs.append(ix)
        lg = jnp.where(lane_f == ix, -jnp.inf, lg)
    exps = [jnp.exp(vv - vals[0]) for vv in vals]
    denom = exps[0] + exps[1] + exps[2] + exps[3]
    idx_out = jnp.zeros(logits.shape, F32)
    tw_out = jnp.zeros(logits.shape, F32)
    for kk in range(TOP_K):
        idx_out = jnp.where(lane == kk, idxs[kk], idx_out)
        tw_out = jnp.where(lane == kk, exps[kk] / denom, tw_out)

    @pl.when(pl.program_id(0) == 0)
    def _():
        cnt_scr[...] = jnp.zeros(cnt_scr.shape, F32)

    tm = logits.shape[0]
    hits = jnp.zeros(logits.shape, F32)
    for kk in range(TOP_K):
        hits = hits + jnp.where(lane_f == idxs[kk], 1.0, 0.0)
    ri = lax.broadcasted_iota(jnp.int32, (tm, tm), 0)
    ci = lax.broadcasted_iota(jnp.int32, (tm, tm), 1)
    before = jnp.where(ci < ri, 1.0, 0.0).astype(BF16)
    base = cnt_scr[0:1, :] + _dot(before, hits.astype(BF16))
    for kk in range(TOP_K):
        rank = jnp.sum(jnp.where(lane_f == idxs[kk], base, 0.0), axis=-1, keepdims=True)
        idx_out = jnp.where(lane == TOP_K + kk, rank, idx_out)
    cnt_new = cnt_scr[...] + jnp.sum(hits, axis=0, keepdims=True)
    cnt_scr[...] = cnt_new
    cnt_ref[...] = cnt_new.astype(jnp.int32)
    idx_ref[...] = idx_out.astype(jnp.int32)
    tw_ref[...] = tw_out


def _merge(x2, oa, ob, proj, wa, wb, wo, g1, n2, sc2, sh2, wr, br, seq, n_experts):
    t, d = x2.shape
    tm = min(256, seq)
    tpb = seq // tm
    gw = d
    wr_hi = wr.astype(BF16)
    wr_lo = (wr - wr_hi.astype(F32)).astype(BF16)
    wr2 = jnp.concatenate([wr_hi, wr_lo], axis=1)
    const = lambda shape: pl.BlockSpec(shape, lambda i: tuple(0 for _ in shape),
                                       pipeline_mode=pl.Buffered(1))
    bvec = pl.BlockSpec((1, 1, d), lambda i: (i // tpb, 0, 0))
    kern = functools.partial(_merge_kernel, n_experts=n_experts)
    return pl.pallas_call(
        kern,
        grid=(t // tm,),
        in_specs=[pl.BlockSpec((tm, d), lambda i: (i, 0)),
                  pl.BlockSpec((tm, RWKV_WIDTH), lambda i: (i, 0)),
                  pl.BlockSpec((tm, DIFF_WIDTH), lambda i: (i, 0)),
                  pl.BlockSpec((tm, gw), lambda i: (i, OFF_GA // gw)),
                  pl.BlockSpec((tm, gw), lambda i: (i, OFF_GB // gw)),
                  const(wa.shape), const(wb.shape), const(wo.shape),
                  bvec, const((1, d)), bvec, bvec,
                  const(wr2.shape), const(wr_hi.shape), const(br.shape)],
        out_specs=[pl.BlockSpec((tm, d), lambda i: (i, 0)),
                   pl.BlockSpec((tm, d), lambda i: (i, 0)),
                   pl.BlockSpec((tm, LANES), lambda i: (i, 0)),
                   pl.BlockSpec((tm, LANES), lambda i: (i, 0)),
                   pl.BlockSpec((8, LANES), lambda i: (0, 0))],
        out_shape=[jax.ShapeDtypeStruct((t, d), F32), jax.ShapeDtypeStruct((t, d), F32),
                   jax.ShapeDtypeStruct((t, LANES), jnp.int32),
                   jax.ShapeDtypeStruct((t, LANES), F32),
                   jax.ShapeDtypeStruct((8, LANES), jnp.int32)],
        scratch_shapes=[pltpu.VMEM((8, LANES), F32)],
        compiler_params=_cparams(("arbitrary",)),
        name="merge",
    )(x2, oa, ob, proj, proj, wa, wb, wo, g1, n2, sc2, sh2, wr2, wr_hi, br)


def _row_copy(src_hbm, dst_ref, src_row, dst_row, sem):
    return pltpu.make_async_copy(src_hbm.at[pl.ds(src_row, 1)], dst_ref.at[pl.ds(dst_row, 1)], sem)


def _expert_kernel(be_ref, na_ref, tok0_ref, tokn_ref, h_hbm, wg_ref, bg_ref, wu_ref, bu_ref,
                   wd_ref, bd_ref, ys_ref, xs_scr, xb_scr, sems, *, bm, rows_per_step):
    i = pl.program_id(0)
    f = pl.program_id(1)
    n_act = na_ref[0]
    slot = i % 2

    def wait_slot(s):
        pltpu.make_async_copy(xs_scr.at[s], xs_scr.at[s], sems.at[s]).wait()

    @pl.when(i < n_act)
    def _():
        @pl.when(f == 0)
        def _():
            @pl.when(i == 0)
            def _():
                def issue(r, carry):
                    _row_copy(h_hbm, xs_scr.at[0], tok0_ref[0, 0, r], r, sems.at[0]).start()
                    return carry

                lax.fori_loop(0, bm, issue, 0)

            wait_slot(slot)
            xb_scr[...] = xs_scr[slot].astype(BF16)
            ys_ref[...] = jnp.broadcast_to(bd_ref[0], ys_ref.shape)

        for r in range(rows_per_step):
            _row_copy(h_hbm, xs_scr.at[1 - slot], tokn_ref[0, 0, r], f * rows_per_step + r,
                      sems.at[1 - slot]).start()

        xb = xb_scr[...]
        gt = jnp.minimum(_dot(xb, wg_ref[0].astype(BF16)) + bg_ref[0], SWIGLU_LIMIT)
        up = jnp.clip(_dot(xb, wu_ref[0].astype(BF16)) + bu_ref[0], -SWIGLU_LIMIT, SWIGLU_LIMIT)
        act = (up + 1.0) * gt * jax.nn.sigmoid(SWIGLU_ALPHA * gt)
        ys_ref[...] += _dot(act.astype(BF16), wd_ref[0].astype(BF16))

        @pl.when(jnp.logical_and(f == pl.num_programs(1) - 1, i == n_act - 1))
        def _():
            wait_slot(1 - slot)

    @pl.when(jnp.logical_and(i >= n_act, f == 0))
    def _():
        ys_ref[...] = jnp.zeros(ys_ref.shape, F32)


def _experts(h2, row_token, block_expert, n_active, wg, bg, wu, bu, wd, bd, bm):
    d = h2.shape[1]
    n_exp, _, ff = wg.shape
    tf = 512
    nf = ff // tf
    rows_per_step = bm // nf
    p = row_token.shape[0]
    n_blocks = p // bm

    def row(i, na):
        return jnp.minimum(i, na[0] - 1)

    def next_rows(i, f, be, na):
        return (jnp.minimum(i + 1, n_blocks - 1) * nf + f, 0, 0)

    kern = functools.partial(_expert_kernel, bm=bm, rows_per_step=rows_per_step)
    grid_spec = pltpu.PrefetchScalarGridSpec(
        num_scalar_prefetch=2,
        grid=(n_blocks, nf),
        in_specs=[pl.BlockSpec((1, 1, bm), lambda i, f, be, na: (0, 0, 0), memory_space=pltpu.SMEM),
                  pl.BlockSpec((1, 1, rows_per_step), next_rows, memory_space=pltpu.SMEM),
                  pl.BlockSpec(memory_space=pl.ANY),
                  pl.BlockSpec((1, d, tf), lambda i, f, be, na: (be[row(i, na)], 0, f)),
                  pl.BlockSpec((1, 1, tf), lambda i, f, be, na: (be[row(i, na)], 0, f)),
                  pl.BlockSpec((1, d, tf), lambda i, f, be, na: (be[row(i, na)], 0, f)),
                  pl.BlockSpec((1, 1, tf), lambda i, f, be, na: (be[row(i, na)], 0, f)),
                  pl.BlockSpec((1, tf, d), lambda i, f, be, na: (be[row(i, na)], f, 0)),
                  pl.BlockSpec((1, 1, d), lambda i, f, be, na: (be[row(i, na)], 0, 0))],
        out_specs=pl.BlockSpec((bm, d), lambda i, f, be, na: (i, 0)),
        scratch_shapes=[pltpu.VMEM((2, bm, d), F32), pltpu.VMEM((bm, d), BF16),
                        pltpu.SemaphoreType.DMA((2,))],
    )
    return pl.pallas_call(
        kern,
        grid_spec=grid_spec,
        out_shape=jax.ShapeDtypeStruct((p, d), F32),
        compiler_params=_cparams(("arbitrary", "arbitrary")),
        name="experts",
    )(block_expert, n_active, row_token.reshape(n_blocks, 1, bm),
      row_token.reshape(n_blocks * nf, 1, rows_per_step), h2,
      wg, bg.reshape(n_exp, 1, ff), wu, bu.reshape(n_exp, 1, ff), wd, bd.reshape(n_exp, 1, d))


def _combine_kernel(pos_ref, x1_ref, tw_ref, g2_ref, ys_hbm, o_ref, rows_scr, sem, *, tc):
    def issue(r, carry):
        for kk in range(TOP_K):
            _row_copy(ys_hbm, rows_scr.at[kk], pos_ref[0, 0, r * TOP_K + kk], r, sem).start(
                priority=kk % 2)
        return carry

    lax.fori_loop(0, tc, issue, 0)
    pltpu.make_async_copy(rows_scr, rows_scr, sem).wait()
    tw = tw_ref[...]
    moe = tw[:, 0:1] * rows_scr[0]
    for kk in range(1, TOP_K):
        moe = moe + tw[:, kk:kk + 1] * rows_scr[kk]
    o_ref[...] = x1_ref[...] + g2_ref[0] * moe


def _combine(x1, top_w, g2, ys, pos, seq):
    t, d = x1.shape
    tc = min(128, seq)
    tpb = seq // tc
    n_steps = t // tc
    kern = functools.partial(_combine_kernel, tc=tc)
    return pl.pallas_call(
        kern,
        grid=(n_steps,),
        in_specs=[pl.BlockSpec((1, 1, tc * TOP_K), lambda i: (i, 0, 0), memory_space=pltpu.SMEM),
                  pl.BlockSpec((tc, d), lambda i: (i, 0)),
                  pl.BlockSpec((tc, LANES), lambda i: (i, 0)),
                  pl.BlockSpec((1, 1, d), lambda i: (i // tpb, 0, 0)),
                  pl.BlockSpec(memory_space=pl.ANY)],
        out_specs=pl.BlockSpec((tc, d), lambda i: (i, 0)),
        out_shape=jax.ShapeDtypeStruct((t, d), F32),
        scratch_shapes=[pltpu.VMEM((TOP_K, tc, d), F32), pltpu.SemaphoreType.DMA(())],
        compiler_params=_cparams(("arbitrary",)),
        name="combine",
    )(pos.reshape(n_steps, 1, tc * TOP_K), x1, top_w, g2, ys)


def _route(top_idx, rank, counts, bm):
    t = top_idx.shape[0]
    n_experts = counts.shape[0]
    m = t * TOP_K
    n_blocks = (m + n_experts * (bm - 1) + bm - 1) // bm
    flat_e = top_idx.reshape(m)
    rank = rank.reshape(m)
    padded = (counts + bm - 1) // bm * bm
    pad_end = jnp.cumsum(padded)
    pad_start = pad_end - padded
    dest = (pad_start[flat_e] + rank).astype(jnp.int32)
    row_token = jnp.zeros((n_blocks * bm,), jnp.int32).at[dest].set(
        jnp.arange(m, dtype=jnp.int32) // TOP_K, unique_indices=True)
    block_start = jnp.arange(n_blocks, dtype=jnp.int32) * bm
    block_expert = jnp.minimum(jnp.sum(pad_end[None, :] <= block_start[:, None], axis=1),
                               n_experts - 1).astype(jnp.int32)
    n_active = (pad_end[-1] // bm).astype(jnp.int32).reshape(1)
    return row_token, dest, block_expert, n_active


def _moe(x1, h2, top_idx, top_w, counts, g2, wg, bg, wu, bu, wd, bd, seq):
    n_experts = wg.shape[0]
    bm = 768
    row_token, dest, block_expert, n_active = _route(
        top_idx[:, :TOP_K], top_idx[:, TOP_K:2 * TOP_K], counts[0, :n_experts], bm)
    ys = _experts(h2, row_token, block_expert, n_active, wg, bg, wu, bu, wd, bd, bm)
    return _combine(x1, top_w, g2, ys, dest, seq)


def _layer(x, c, l, w_ada, b_ada, norm1_gain, norm2_gain, w_in, shift_mu, w0, w_decay_up, a0,
           w_aaa_up, w_gate_up, k_k, k_a, r_k, gn_w, gn_b, q_gain, k_gain, lam_q1, lam_k1,
           lam_q2, lam_k2, subln_gain, w_branch_a, w_branch_b, w_out, w_router, b_router,
           w_exp_gate, b_exp_gate, w_exp_up, b_exp_up, w_exp_down, b_exp_down):
    bsz, seq, d = x.shape
    t = bsz * seq
    mod = _ada(c, w_ada, b_ada)
    sh1, sc1, g1, sh2, sc2, g2 = (m.reshape(bsz, 1, d) for m in jnp.split(mod, 6, axis=-1))

    decay_rank, aaa_rank, gate_rank = w_decay_up.shape[0], w_aaa_up.shape[0], w_gate_up.shape[0]
    w_packed, mu_packed = _pack_in_weights(w_in, shift_mu, decay_rank, aaa_rank, gate_rank)
    x2 = x.reshape(t, d)
    proj = _inproj(x2, norm1_gain.reshape(1, d), sc1, sh1, w_packed, mu_packed, seq)
    proj3 = proj.reshape(bsz, seq, PACKED_WIDTH)

    padr = lambda w: jnp.pad(w, ((0, LANES - w.shape[0]), (0, 0))).astype(BF16)
    prm = jnp.stack([w0, a0, k_k, k_a, r_k, gn_w, gn_b, jnp.zeros_like(w0)], axis=0)
    o_a = _rwkv(proj3, padr(w_decay_up), padr(w_aaa_up), w_gate_up.astype(BF16), prm, seq)

    lambda_init = 0.8 - 0.6 * math.exp(-0.3 * l)
    q2, kh, vh = _attn_prep(proj3, q_gain, k_gain, seq)
    lam_rows = jnp.pad(jnp.stack([lam_q1, lam_k1, lam_q2, lam_k2], axis=0),
                       ((0, 4), (0, LANES - DIFF_QK_DIM)))
    o_b = _flash(q2, kh, vh, lam_rows, subln_gain, seq, lambda_init)

    n_experts = w_router.shape[1]
    wr = jnp.pad(w_router, ((0, 0), (0, LANES - n_experts)))
    br = jnp.pad(b_router, (0, LANES - n_experts)).reshape(1, LANES)
    x1, h2, top_idx, top_w, counts = _merge(
        x2, o_a.reshape(t, RWKV_WIDTH), o_b.reshape(t, DIFF_WIDTH), proj,
        w_branch_a.astype(BF16), w_branch_b.astype(BF16), w_out.astype(BF16),
        g1, norm2_gain.reshape(1, d), sc2, sh2, wr, br, seq, n_experts)

    out = _moe(x1, h2, top_idx, top_w, counts, g2, w_exp_gate, b_exp_gate, w_exp_up, b_exp_up,
               w_exp_down, b_exp_down, seq)
    return out.reshape(bsz, seq, d)


def kernel(x, c, w_ada, b_ada, norm1_gain, norm2_gain, w_in, shift_mu, w0, w_decay_up, a0, w_aaa_up, w_gate_up, k_k, k_a, r_k, gn_w, gn_b, q_gain, k_gain, lam_q1, lam_k1, lam_q2, lam_k2, subln_gain, w_branch_a, w_branch_b, w_out, w_router, b_router, w_exp_gate, b_exp_gate, w_exp_up, b_exp_up, w_exp_down, b_exp_down):
    per_layer = (w_ada, b_ada, norm1_gain, norm2_gain, w_in, shift_mu, w0, w_decay_up, a0,
                 w_aaa_up, w_gate_up, k_k, k_a, r_k, gn_w, gn_b, q_gain, k_gain, lam_q1, lam_k1,
                 lam_q2, lam_k2, subln_gain, w_branch_a, w_branch_b, w_out, w_router, b_router,
                 w_exp_gate, b_exp_gate, w_exp_up, b_exp_up, w_exp_down, b_exp_down)
    for l in range(w_in.shape[0]):
        x = _layer(x, c, l, *(p[l] for p in per_layer))
    return x
```

```python
import functools
import math

import jax
import jax.numpy as jnp
from jax import lax
from jax.experimental import pallas as pl
from jax.experimental.pallas import tpu as pltpu

F32 = jnp.float32
BF16 = jnp.bfloat16

NORM_EPS = 1e-6
CHUNK = 64

RWKV_HEADS = 16
RWKV_HEAD_DIM = 64
RWKV_WIDTH = RWKV_HEADS * RWKV_HEAD_DIM
RWKV_GN_EPS = 64e-5

DIFF_HEADS = 8
DIFF_QK_DIM = 64
DIFF_V_DIM = 128
DIFF_WIDTH = DIFF_HEADS * DIFF_V_DIM
ROPE_THETA = 500000.0
ROPE_DIM = DIFF_QK_DIM // 4

TOP_K = 4
SWIGLU_LIMIT = 7.0
SWIGLU_ALPHA = 1.702

LANES = 128
LORA_PACK = 512
VMEM_LIMIT = 56 * 1024 * 1024

ADA_COLS = 1024
INPROJ_ROWS, INPROJ_COLS = 1024, 1536
RWKV_ROWS = 512
PREP_ROWS = 512
FLASH_BLOCK = 1024
MERGE_ROWS = 256
EXPERT_ROWS, EXPERT_FF_COLS = 768, 512
COMBINE_ROWS = 128

OFF_R, OFF_K, OFF_V = 0, 1024, 2048
OFF_QD, OFF_KD, OFF_VD = 3072, 4096, 5120
OFF_GA, OFF_GB = 6144, 8192
OFF_LORA = 10240
PACKED_WIDTH = OFF_LORA + LORA_PACK


def _cparams(sem):
    return pltpu.CompilerParams(dimension_semantics=sem, vmem_limit_bytes=VMEM_LIMIT)


def _nt_dot(a, b):
    return lax.dot_general(a, b, (((1,), (1,)), ((), ())), preferred_element_type=F32)


def _dot(a, b):
    return jnp.dot(a, b, preferred_element_type=F32)


def _ada_kernel(c_ref, w_ref, b_ref, o_ref):
    c = c_ref[...]
    s = c * jax.nn.sigmoid(c)
    o_ref[...] = _dot(s.astype(BF16), w_ref[...].astype(BF16)) + b_ref[...]


def _ada(c, w, b):
    bsz, d = c.shape
    n = w.shape[1]
    rows = 8
    tn = ADA_COLS
    cp = jnp.pad(c, ((0, rows - bsz), (0, 0)))
    out = pl.pallas_call(
        _ada_kernel,
        grid=(n // tn,),
        in_specs=[pl.BlockSpec((rows, d), lambda j: (0, 0)),
                  pl.BlockSpec((d, tn), lambda j: (0, j)),
                  pl.BlockSpec((1, tn), lambda j: (0, j))],
        out_specs=pl.BlockSpec((rows, tn), lambda j: (0, j)),
        out_shape=jax.ShapeDtypeStruct((rows, n), F32),
        compiler_params=_cparams(("arbitrary",)),
        name="ada",
    )(cp, w, b.reshape(1, n))
    return out[:bsz]


def _inproj_kernel(x_ref, g_ref, sc_ref, sh_ref, w_ref, mu_ref, o_ref, h_scr, carry_scr,
                   *, tiles_per_batch, n_rkv_tiles, lora_tile):
    i = pl.program_id(0)
    j = pl.program_id(1)

    @pl.when(j == 0)
    def _():
        x = x_ref[...]
        ms = jnp.mean(x * x, axis=-1, keepdims=True)
        y = x * lax.rsqrt(ms + NORM_EPS) * g_ref[...]
        h_scr[...] = (y * (1.0 + sc_ref[0]) + sh_ref[0]).astype(BF16)

    p = _dot(h_scr[...], w_ref[...])
    shifted = jnp.logical_or(j < n_rkv_tiles, j == lora_tile)

    @pl.when(shifted)
    def _():
        slot = jnp.where(j == lora_tile, n_rkv_tiles, j)

        @pl.when(i % tiles_per_batch == 0)
        def _():
            carry_scr[slot] = jnp.zeros(carry_scr.shape[1:], F32)

        prev_last = carry_scr[slot]
        tm = p.shape[0]
        carry_scr[slot] = p[tm - 1:tm, :]
        row = lax.broadcasted_iota(jnp.int32, p.shape, 0)
        prev = jnp.where(row == 0, prev_last, pltpu.roll(p, 1, 0))
        o_ref[...] = (p + (prev - p) * mu_ref[...]).astype(o_ref.dtype)

    @pl.when(jnp.logical_not(shifted))
    def _():
        o_ref[...] = p.astype(o_ref.dtype)


def _inproj(x2, gain, sc, sh, w_packed, mu_packed, seq):
    t, d = x2.shape
    tm = min(INPROJ_ROWS, seq)
    tn = INPROJ_COLS
    assert seq % tm == 0 and t % tm == 0
    assert PACKED_WIDTH % tn == 0 and OFF_QD % tn == 0 and PACKED_WIDTH - tn <= OFF_LORA
    n_col = PACKED_WIDTH // tn
    n_rkv_tiles = OFF_QD // tn
    lora_tile = n_col - 1
    tiles_per_batch = seq // tm

    kern = functools.partial(_inproj_kernel, tiles_per_batch=tiles_per_batch,
                             n_rkv_tiles=n_rkv_tiles, lora_tile=lora_tile)
    return pl.pallas_call(
        kern,
        grid=(t // tm, n_col),
        in_specs=[pl.BlockSpec((tm, d), lambda i, j: (i, 0)),
                  pl.BlockSpec((1, d), lambda i, j: (0, 0)),
                  pl.BlockSpec((1, 1, d), lambda i, j: (i // tiles_per_batch, 0, 0)),
                  pl.BlockSpec((1, 1, d), lambda i, j: (i // tiles_per_batch, 0, 0)),
                  pl.BlockSpec((d, tn), lambda i, j: (0, j)),
                  pl.BlockSpec((1, tn), lambda i, j: (0, j))],
        out_specs=pl.BlockSpec((tm, tn), lambda i, j: (i, j)),
        out_shape=jax.ShapeDtypeStruct((t, PACKED_WIDTH), BF16),
        scratch_shapes=[pltpu.VMEM((tm, d), BF16),
                        pltpu.VMEM((n_rkv_tiles + 1, 1, tn), F32)],
        compiler_params=_cparams(("arbitrary", "arbitrary")),
        name="inproj",
    )(x2, gain, sc, sh, w_packed, mu_packed)


def _pack_in_weights(w_in, shift_mu, decay_rank, aaa_rank, gate_rank):
    assert decay_rank <= LANES and aaa_rank <= LANES and gate_rank == 2 * LANES
    rw = RWKV_WIDTH
    o = 3 * rw
    w_r, w_k, w_v = w_in[:, 0:rw], w_in[:, rw:2 * rw], w_in[:, 2 * rw:3 * rw]
    w_dl = w_in[:, o:o + decay_rank]
    w_al = w_in[:, o + decay_rank:o + decay_rank + aaa_rank]
    w_gl = w_in[:, o + decay_rank + aaa_rank:o + decay_rank + aaa_rank + gate_rank]
    rest = w_in[:, o + decay_rank + aaa_rank + gate_rank:]

    def padc(a, width):
        return jnp.pad(a, ((0, 0), (0, width - a.shape[1])))

    packed = jnp.concatenate(
        [w_r, w_k, w_v, rest, padc(w_dl, LANES), padc(w_al, LANES), w_gl], axis=1).astype(BF16)
    assert packed.shape[1] == PACKED_WIDTH
    mu = shift_mu.reshape(1, -1)
    mu_packed = jnp.concatenate(
        [mu[:, :o], jnp.zeros((1, OFF_LORA - o), F32), padc(mu[:, o:o + decay_rank], LANES),
         padc(mu[:, o + decay_rank:o + decay_rank + aaa_rank], LANES),
         mu[:, o + decay_rank + aaa_rank:]], axis=1)
    return packed, mu_packed


def _split3_bf16(x):
    hi = x.astype(BF16)
    r1 = x - hi.astype(F32)
    mid = r1.astype(BF16)
    lo = (r1 - mid.astype(F32)).astype(BF16)
    return hi, mid, lo


def _rwkv_kernel(r_ref, k_ref, v_ref, lora_ref, wd_ref, wa_ref, wg_ref, prm_ref, o_ref, st_ref,
                 *, tm, nb):
    c = CHUNK
    n = RWKV_HEAD_DIM
    n_chunks = tm // c

    @pl.when(pl.program_id(1) == 0)
    def _():
        st_ref[...] = jnp.zeros(st_ref.shape, F32)

    lane = lax.broadcasted_iota(jnp.int32, (1, LANES), 1)
    m0 = lane < n
    m0_256 = jnp.concatenate([m0, m0], axis=1)

    def head_sum(x):
        s0 = jnp.sum(jnp.where(m0, x, 0.0), axis=-1, keepdims=True)
        s1 = jnp.sum(jnp.where(m0, 0.0, x), axis=-1, keepdims=True)
        return jnp.where(m0, s0, s1)

    prm = prm_ref[...]
    w0, a0, k_k, k_a, r_k, gn_w, gn_b = (prm[i:i + 1] for i in range(7))

    tb = min(tm, 2 * LANES)
    ri = lax.broadcasted_iota(jnp.int32, (tb, tb), 0)
    ci = lax.broadcasted_iota(jnp.int32, (tb, tb), 1)
    tri = jnp.where(jnp.logical_and(ci <= ri, ci // c == ri // c), 1.0, 0.0).astype(BF16)

    def prep(bb):
        r = r_ref[bb].astype(F32)
        k = k_ref[bb].astype(F32)
        v = v_ref[bb].astype(F32)
        lora = lora_ref[bb]
        d_code = lora[:, 0:LANES].astype(F32)
        a_code = lora[:, LANES:2 * LANES]
        g_code = lora[:, 2 * LANES:4 * LANES].astype(F32)

        wl = w0 + _dot(jnp.tanh(d_code).astype(BF16), wd_ref[...])
        z = -wl
        softplus = jnp.maximum(z, 0.0) + jnp.log(1.0 + jnp.exp(-jnp.abs(z)))
        lw = -jnp.exp(-softplus - 0.5)
        a = jax.nn.sigmoid(a0 + _dot(a_code, wa_ref[...]))
        g = _dot(jax.nn.sigmoid(g_code).astype(BF16), wg_ref[...])

        kk = k * k_k
        kk = kk / jnp.maximum(jnp.sqrt(head_sum(kk * kk)), 1e-12)
        k2 = k * (1.0 + (a - 1.0) * k_a)
        a_vec = -kk
        b_vec = kk * a
        bonus = head_sum(r * k2 * r_k) * v

        lw3 = jnp.concatenate(_split3_bf16(lw), axis=1)
        cum3 = jnp.concatenate([_dot(tri, lw3[u * tb:(u + 1) * tb]) for u in range(tm // tb)], axis=0)
        cum = cum3[:, 0:LANES] + cum3[:, LANES:2 * LANES] + cum3[:, 2 * LANES:3 * LANES]
        cum_end = jnp.concatenate(
            [jnp.broadcast_to(cum[(q + 1) * c - 1:(q + 1) * c, :], (c, LANES)) for q in range(n_chunks)],
            axis=0)
        e_neg = jnp.exp(-cum)
        e_end = jnp.exp(cum_end - cum)
        return dict(a_t=a_vec * jnp.exp(cum - lw), r_t=r * jnp.exp(cum), k_h=k2 * e_neg, b_h=b_vec * e_neg,
                    k_d=k2 * e_end, b_d=b_vec * e_end, v=v, d_end=jnp.exp(cum_end), bonus=bonus, g=g)

    preps = [prep(bb) for bb in range(nb)]

    row64 = lax.broadcasted_iota(jnp.int32, (c, LANES), 0)
    col64 = lax.broadcasted_iota(jnp.int32, (c, LANES), 1) % n
    strict = row64 > col64
    incl = row64 >= col64
    rb = lax.broadcasted_iota(jnp.int32, (LANES, LANES), 0) < n
    cb = lax.broadcasted_iota(jnp.int32, (LANES, LANES), 1) < n
    blockdiag = rb == cb

    def stack_heads(zz, mask):
        return jnp.concatenate([jnp.where(mask, zz, 0.0), jnp.where(mask, 0.0, zz)],
                               axis=0).astype(BF16)

    items = [(q, bb) for q in range(n_chunks) for bb in range(nb)]
    qs = range(len(items))
    ch = lambda name: [preps[bb][name][q * c:(q + 1) * c] for q, bb in items]
    at_c, rt_c, kh_c, bh_c, kd_c, bd_c, v_c = (
        ch(name) for name in ("a_t", "r_t", "k_h", "b_h", "k_d", "b_d", "v"))
    out0 = [_nt_dot(jnp.concatenate([jnp.where(m0, at_c[q], 0.0), jnp.where(m0, rt_c[q], 0.0)],
                                    axis=0).astype(BF16),
                    jnp.concatenate([bh_c[q], kh_c[q]], axis=0).astype(BF16)) for q in qs]
    out1 = [_nt_dot(jnp.concatenate([jnp.where(m0, 0.0, at_c[q]), jnp.where(m0, 0.0, rt_c[q])],
                                    axis=0).astype(BF16),
                    jnp.concatenate([kh_c[q], bh_c[q]], axis=0).astype(BF16)) for q in qs]
    x_cat = [jnp.where(strict, jnp.where(m0, out0[q][:c], out1[q][:c]), 0.0) for q in qs]
    ak_x = [jnp.where(strict, jnp.where(m0, out1[q][:c], out0[q][:c]), 0.0) for q in qs]
    rb_cat = [jnp.where(incl, jnp.where(m0, out0[q][c:], out1[q][c:]), 0.0) for q in qs]
    rk_x = [jnp.where(incl, jnp.where(m0, out1[q][c:], out0[q][c:]), 0.0) for q in qs]
    w12 = [_dot(jnp.concatenate([ak_x[q], rk_x[q]], axis=0).astype(BF16),
                jnp.concatenate([jnp.where(m0, 0.0, v_c[q]), jnp.where(m0, v_c[q], 0.0)],
                                axis=0).astype(BF16)) for q in qs]

    zz = [jnp.concatenate([at_c[q], w12[q][:c]], axis=1) for q in qs]
    xp = x_cat
    for step in range(6):
        zz = [zz[q] + _dot(xp[q].astype(BF16), stack_heads(zz[q], m0_256)) for q in qs]
        if step < 5:
            xp = [_dot(xp[q].astype(BF16), stack_heads(xp[q], m0)) for q in qs]
    rbz = [_dot(rb_cat[q].astype(BF16), stack_heads(zz[q], m0_256)) for q in qs]
    q_c = [rt_c[q] + rbz[q][:, 0:LANES] for q in qs]
    z_c = [w12[q][c:] + rbz[q][:, LANES:] for q in qs]
    mn = []
    for q in qs:
        t_src = jnp.concatenate(
            [zz[q], jnp.concatenate([jnp.zeros_like(v_c[q]), v_c[q]], axis=1)], axis=0)
        mn.append(_dot(t_src.T.astype(BF16),
                       jnp.concatenate([bd_c[q], kd_c[q]], axis=0).astype(BF16)))
    m_x = [jnp.where(blockdiag, mn[q][0:LANES], 0.0).astype(BF16) for q in qs]
    n_x = [jnp.where(blockdiag, mn[q][LANES:], 0.0) for q in qs]

    states = [st_ref[bb] for bb in range(nb)]
    ys = [[] for _ in range(nb)]
    for j, (q, bb) in enumerate(items):
        s_bf = states[bb].astype(BF16)
        ys[bb].append(_nt_dot(q_c[j].astype(BF16), s_bf) + z_c[j])
        states[bb] = (states[bb] * preps[bb]["d_end"][q * c:q * c + 1, :]
                      + _dot(s_bf, m_x[j]) + n_x[j])

    for bb in range(nb):
        st_ref[bb] = states[bb]
        y = jnp.concatenate(ys[bb], axis=0)
        mu = head_sum(y) * (1.0 / n)
        yc = y - mu
        var = head_sum(yc * yc) * (1.0 / n)
        yn = yc * lax.rsqrt(var + RWKV_GN_EPS) * gn_w + gn_b
        o_ref[bb] = ((yn + preps[bb]["bonus"]) * preps[bb]["g"]).astype(o_ref.dtype)


def _rwkv(proj3, wd, wa, wg, prm, seq):
    bsz = proj3.shape[0]
    tm = min(RWKV_ROWS, seq)
    assert seq % tm == 0 and tm % CHUNK == 0
    n_pairs = RWKV_WIDTH // LANES
    blk = lambda off: (lambda p, t: (0, t, off // LANES + p))
    kern = functools.partial(_rwkv_kernel, tm=tm, nb=bsz)
    return pl.pallas_call(
        kern,
        grid=(n_pairs, seq // tm),
        in_specs=[pl.BlockSpec((bsz, tm, LANES), blk(OFF_R)),
                  pl.BlockSpec((bsz, tm, LANES), blk(OFF_K)),
                  pl.BlockSpec((bsz, tm, LANES), blk(OFF_V)),
                  pl.BlockSpec((bsz, tm, LORA_PACK), lambda p, t: (0, t, OFF_LORA // LORA_PACK)),
                  pl.BlockSpec((LANES, LANES), lambda p, t: (0, p)),
                  pl.BlockSpec((LANES, LANES), lambda p, t: (0, p)),
                  pl.BlockSpec((2 * LANES, LANES), lambda p, t: (0, p)),
                  pl.BlockSpec((8, LANES), lambda p, t: (0, p))],
        out_specs=pl.BlockSpec((bsz, tm, LANES), lambda p, t: (0, t, p)),
        out_shape=jax.ShapeDtypeStruct((bsz, seq, RWKV_WIDTH), BF16),
        scratch_shapes=[pltpu.VMEM((bsz, LANES, LANES), F32)],
        compiler_params=_cparams(("arbitrary", "arbitrary")),
        name="rwkv",
    )(proj3, proj3, proj3, proj3, wd, wa, wg, prm)


def _attn_prep_kernel(q_ref, k_ref, v_ref, cos_ref, sa_ref, sb_ref, gavg_ref, qg_ref, kg_ref,
                      q2_ref, ko_ref, vo_ref):
    width = q_ref.shape[-1]
    reps = width // LANES
    tile = lambda t: jnp.concatenate([t] * reps, axis=1)
    cos_t, sin_a, sin_b = tile(cos_ref[...]), tile(sa_ref[...]), tile(sb_ref[...])
    gavg = gavg_ref[...]
    half = ROPE_DIM // 2

    def norm_rope(x, gain):
        x = x.astype(F32)
        sq = x * x
        hi = sq.astype(BF16)
        lo = (sq - hi.astype(F32)).astype(BF16)
        gw = gavg.shape[0]
        ms = jnp.concatenate(
            [_dot(hi[:, u * gw:(u + 1) * gw], gavg) + _dot(lo[:, u * gw:(u + 1) * gw], gavg)
             for u in range(width // gw)], axis=1)
        y = x * lax.rsqrt(ms + NORM_EPS) * gain
        return (y * cos_t + pltpu.roll(y, width - half, 1) * sin_a + pltpu.roll(y, half, 1) * sin_b)

    qn = norm_rope(q_ref[0], qg_ref[...])
    kn = norm_rope(k_ref[0], kg_ref[...])
    lane = lax.broadcasted_iota(jnp.int32, (1, LANES), 1)
    m0 = lane < DIFF_QK_DIM
    for h in range(DIFF_HEADS):
        sl = slice(h * LANES, (h + 1) * LANES)
        qh = qn[:, sl]
        q2_ref[0, h, 0] = jnp.where(m0, qh, 0.0).astype(BF16)
        q2_ref[0, h, 1] = jnp.where(m0, 0.0, qh).astype(BF16)
        ko_ref[0, h] = kn[:, sl].astype(BF16)
        vo_ref[0, h] = v_ref[0, :, sl]


def _rope_tables(seq):
    half = ROPE_DIM // 2
    inv_freq = ROPE_THETA ** (-jnp.arange(half, dtype=F32) * 2.0 / ROPE_DIM)
    ang = jnp.arange(seq, dtype=jnp.int32).astype(F32)[:, None] * inv_freq[None, :]
    cos, sin = jnp.cos(ang), jnp.sin(ang)
    ones = jnp.ones((seq, DIFF_QK_DIM - ROPE_DIM), F32)
    zeros = jnp.zeros((seq, DIFF_QK_DIM - ROPE_DIM), F32)
    zh = jnp.zeros((seq, half), F32)
    cos64 = jnp.concatenate([cos, cos, ones], axis=1)
    sa64 = jnp.concatenate([-sin, zh, zeros], axis=1)
    sb64 = jnp.concatenate([zh, sin, zeros], axis=1)
    dup = lambda t: jnp.concatenate([t, t], axis=1)
    return dup(cos64), dup(sa64), dup(sb64)


def _attn_prep(proj3, q_gain, k_gain, seq):
    bsz = proj3.shape[0]
    tm = min(PREP_ROWS, seq)
    assert seq % tm == 0
    width = DIFF_HEADS * 2 * DIFF_QK_DIM
    cos_t, sin_a, sin_b = _rope_tables(seq)
    gi = jnp.arange(2 * LANES) // DIFF_QK_DIM
    gavg = jnp.where(gi[:, None] == gi[None, :], 1.0 / DIFF_QK_DIM, 0.0).astype(BF16)
    reps = width // DIFF_QK_DIM
    qg = (jnp.tile(q_gain, reps) * (DIFF_QK_DIM ** -0.5 * math.log2(math.e))).reshape(1, width)
    kg = jnp.tile(k_gain, reps).reshape(1, width)
    hd = (bsz, DIFF_HEADS, seq, LANES)
    tab = pl.BlockSpec((tm, LANES), lambda b, t: (t, 0))
    return pl.pallas_call(
        _attn_prep_kernel,
        grid=(bsz, seq // tm),
        in_specs=[pl.BlockSpec((1, tm, width), lambda b, t: (b, t, OFF_QD // width)),
                  pl.BlockSpec((1, tm, width), lambda b, t: (b, t, OFF_KD // width)),
                  pl.BlockSpec((1, tm, width), lambda b, t: (b, t, OFF_VD // width)),
                  tab, tab, tab,
                  pl.BlockSpec((2 * LANES, 2 * LANES), lambda b, t: (0, 0)),
                  pl.BlockSpec((1, width), lambda b, t: (0, 0)),
                  pl.BlockSpec((1, width), lambda b, t: (0, 0))],
        out_specs=[pl.BlockSpec((1, DIFF_HEADS, 2, tm, LANES), lambda b, t: (b, 0, 0, t, 0)),
                   pl.BlockSpec((1, DIFF_HEADS, tm, LANES), lambda b, t: (b, 0, t, 0)),
                   pl.BlockSpec((1, DIFF_HEADS, tm, LANES), lambda b, t: (b, 0, t, 0))],
        out_shape=[jax.ShapeDtypeStruct((bsz, DIFF_HEADS, 2, seq, LANES), BF16),
                   jax.ShapeDtypeStruct(hd, BF16), jax.ShapeDtypeStruct(hd, BF16)],
        compiler_params=_cparams(("arbitrary", "arbitrary")),
        name="attn_prep",
    )(proj3, proj3, proj3, cos_t, sin_a, sin_b, gavg, qg, kg)


def _flash_kernel(qi_ref, kj_ref, q_ref, k_ref, v_ref, lam_ref, sg_ref, o_ref, m_ref, acc_ref,
                  *, tq, tk, lambda_init):
    pidx = pl.program_id(2)
    i = qi_ref[pidx]
    j = kj_ref[pidx]

    @pl.when(j == 0)
    def _():
        m_ref[...] = jnp.full(m_ref.shape, -jnp.inf, F32)
        acc_ref[...] = jnp.zeros(acc_ref.shape, F32)

    q = q_ref[0, 0].reshape(2 * tq, LANES)
    s = _nt_dot(q, k_ref[0, 0]).astype(BF16)

    def update(sc):
        m_old = m_ref[...]
        m_new = jnp.maximum(m_old, jnp.max(sc, axis=-1, keepdims=True).astype(F32))
        alpha = jnp.exp2(m_old - m_new)
        m_b = m_new.astype(BF16)
        p = jnp.exp2(sc - jnp.concatenate([m_b] * (tk // LANES), axis=1))
        v = v_ref[0, 0]
        v_ext = jnp.concatenate([v, jnp.ones_like(v)], axis=1)
        acc_ref[...] = jnp.concatenate([alpha, alpha], axis=1) * acc_ref[...] + _dot(p, v_ext)
        m_ref[...] = m_new

    @pl.when(j < i)
    def _():
        update(s)

    @pl.when(j == i)
    def _():
        qpos = lax.broadcasted_iota(jnp.int32, s.shape, 0) % tq
        kpos = lax.broadcasted_iota(jnp.int32, s.shape, 1)
        update(jnp.where(kpos // CHUNK <= qpos // CHUNK, s, -jnp.inf).astype(BF16))
        lam_rows = lam_ref[...]
        lam = (jnp.exp(jnp.sum(lam_rows[0:1] * lam_rows[1:2], axis=-1, keepdims=True))
               - jnp.exp(jnp.sum(lam_rows[2:3] * lam_rows[3:4], axis=-1, keepdims=True))
               + lambda_init)
        acc = acc_ref[...]
        o1 = acc[0:tq, 0:LANES] / acc[0:tq, LANES:]
        o2 = acc[tq:, 0:LANES] / acc[tq:, LANES:]
        o = o1 - lam * o2
        ms = jnp.mean(o * o, axis=-1, keepdims=True)
        o_ref[0] = (o * lax.rsqrt(ms + NORM_EPS) * sg_ref[...] * (1.0 - lambda_init)).astype(o_ref.dtype)


def _flash(q2, kh, vh, lam_rows, subln_gain, seq, lambda_init):
    bsz = q2.shape[0]
    tq = tk = min(FLASH_BLOCK, seq)
    assert seq % tq == 0 and tq % CHUNK == 0
    nq = seq // tq
    pairs = [(i, j) for i in range(nq) for j in range(i + 1)]
    qi = jnp.asarray([p[0] for p in pairs], jnp.int32)
    kj = jnp.asarray([p[1] for p in pairs], jnp.int32)
    kern = functools.partial(_flash_kernel, tq=tq, tk=tk, lambda_init=lambda_init)
    grid_spec = pltpu.PrefetchScalarGridSpec(
        num_scalar_prefetch=2,
        grid=(bsz, DIFF_HEADS, len(pairs)),
        in_specs=[pl.BlockSpec((1, 1, 2, tq, LANES), lambda b, h, p, qi, kj: (b, h, 0, qi[p], 0)),
                  pl.BlockSpec((1, 1, tk, LANES), lambda b, h, p, qi, kj: (b, h, kj[p], 0)),
                  pl.BlockSpec((1, 1, tk, LANES), lambda b, h, p, qi, kj: (b, h, kj[p], 0)),
                  pl.BlockSpec((8, LANES), lambda b, h, p, qi, kj: (0, 0)),
                  pl.BlockSpec((1, LANES), lambda b, h, p, qi, kj: (0, 0))],
        out_specs=pl.BlockSpec((1, tq, LANES), lambda b, h, p, qi, kj: (b, qi[p], h)),
        scratch_shapes=[pltpu.VMEM((2 * tq, LANES), F32), pltpu.VMEM((2 * tq, 2 * LANES), F32)],
    )
    return pl.pallas_call(
        kern,
        grid_spec=grid_spec,
        out_shape=jax.ShapeDtypeStruct((bsz, seq, DIFF_WIDTH), BF16),
        compiler_params=_cparams(("arbitrary", "arbitrary", "arbitrary")),
        name="flash",
    )(qi, kj, q2, kh, vh, lam_rows, subln_gain.reshape(1, LANES))


def _merge_kernel(x_ref, oa_ref, ob_ref, ga_ref, gb_ref, wa_ref, wb_ref, wo_ref, g1_ref,
                  n2_ref, sc2_ref, sh2_ref, wr2_ref, wrh_ref, br_ref,
                  x1_ref, h2_ref, idx_ref, tw_ref, cnt_ref, cnt_scr, *, n_experts):
    ma = _dot(oa_ref[...], wa_ref[...])
    mb = _dot(ob_ref[...], wb_ref[...])
    merged = (jax.nn.sigmoid(ga_ref[...].astype(F32)) * ma
              + jax.nn.sigmoid(gb_ref[...].astype(F32)) * mb)
    x1 = x_ref[...] + g1_ref[0] * _dot(merged.astype(BF16), wo_ref[...])
    x1_ref[...] = x1
    ms = jnp.mean(x1 * x1, axis=-1, keepdims=True)
    h2 = x1 * lax.rsqrt(ms + NORM_EPS) * n2_ref[...] * (1.0 + sc2_ref[0]) + sh2_ref[0]
    h2_ref[...] = h2

    h_hi = h2.astype(BF16)
    h_lo = (h2 - h_hi.astype(F32)).astype(BF16)
    hh = _dot(h_hi, wr2_ref[...])
    logits = hh[:, 0:LANES] + hh[:, LANES:] + _dot(h_lo, wrh_ref[...]) + br_ref[...]
    lane = lax.broadcasted_iota(jnp.int32, logits.shape, 1)
    lane_f = lane.astype(F32)
    lg = jnp.where(lane < n_experts, logits, -jnp.inf)
    vals, idxs = [], []
    for _ in range(TOP_K):
        mx = jnp.max(lg, axis=-1, keepdims=True)
        ix = jnp.min(jnp.where(lg == mx, lane_f, float(LANES)), axis=-1, keepdims=True)
        vals.append(mx)
        idxs.append(ix)
        lg = jnp.where(lane_f == ix, -jnp.inf, lg)
    exps = [jnp.exp(vv - vals[0]) for vv in vals]
    denom = exps[0] + exps[1] + exps[2] + exps[3]
    idx_out = jnp.zeros(logits.shape, F32)
    tw_out = jnp.zeros(logits.shape, F32)
    for kk in range(TOP_K):
        idx_out = jnp.where(lane == kk, idxs[kk], idx_out)
        tw_out = jnp.where(lane == kk, exps[kk] / denom, tw_out)

    @pl.when(pl.program_id(0) == 0)
    def _():
        cnt_scr[...] = jnp.zeros(cnt_scr.shape, F32)

    tm = logits.shape[0]
    hits = jnp.zeros(logits.shape, F32)
    for kk in range(TOP_K):
        hits = hits + jnp.where(lane_f == idxs[kk], 1.0, 0.0)
    ri = lax.broadcasted_iota(jnp.int32, (tm, tm), 0)
    ci = lax.broadcasted_iota(jnp.int32, (tm, tm), 1)
    before = jnp.where(ci < ri, 1.0, 0.0).astype(BF16)
    base = cnt_scr[0:1, :] + _dot(before, hits.astype(BF16))
    for kk in range(TOP_K):
        rank = jnp.sum(jnp.where(lane_f == idxs[kk], base, 0.0), axis=-1, keepdims=True)
        idx_out = jnp.where(lane == TOP_K + kk, rank, idx_out)
    cnt_new = cnt_scr[...] + jnp.sum(hits, axis=0, keepdims=True)
    cnt_scr[...] = cnt_new
    cnt_ref[...] = cnt_new.astype(jnp.int32)
    idx_ref[...] = idx_out.astype(jnp.int32)
    tw_ref[...] = tw_out


def _merge(x2, oa, ob, proj, wa, wb, wo, g1, n2, sc2, sh2, wr, br, seq, n_experts):
    t, d = x2.shape
    tm = min(MERGE_ROWS, seq)
    assert seq % tm == 0
    tpb = seq // tm
    gw = d
    wr_hi = wr.astype(BF16)
    wr_lo = (wr - wr_hi.astype(F32)).astype(BF16)
    wr2 = jnp.concatenate([wr_hi, wr_lo], axis=1)
    const = lambda shape: pl.BlockSpec(shape, lambda i: tuple(0 for _ in shape),
                                       pipeline_mode=pl.Buffered(1))
    bvec = pl.BlockSpec((1, 1, d), lambda i: (i // tpb, 0, 0))
    kern = functools.partial(_merge_kernel, n_experts=n_experts)
    return pl.pallas_call(
        kern,
        grid=(t // tm,),
        in_specs=[pl.BlockSpec((tm, d), lambda i: (i, 0)),
                  pl.BlockSpec((tm, RWKV_WIDTH), lambda i: (i, 0)),
                  pl.BlockSpec((tm, DIFF_WIDTH), lambda i: (i, 0)),
                  pl.BlockSpec((tm, gw), lambda i: (i, OFF_GA // gw)),
                  pl.BlockSpec((tm, gw), lambda i: (i, OFF_GB // gw)),
                  const(wa.shape), const(wb.shape), const(wo.shape),
                  bvec, const((1, d)), bvec, bvec,
                  const(wr2.shape), const(wr_hi.shape), const(br.shape)],
        out_specs=[pl.BlockSpec((tm, d), lambda i: (i, 0)),
                   pl.BlockSpec((tm, d), lambda i: (i, 0)),
                   pl.BlockSpec((tm, LANES), lambda i: (i, 0)),
                   pl.BlockSpec((tm, LANES), lambda i: (i, 0)),
                   pl.BlockSpec((8, LANES), lambda i: (0, 0))],
        out_shape=[jax.ShapeDtypeStruct((t, d), F32), jax.ShapeDtypeStruct((t, d), F32),
                   jax.ShapeDtypeStruct((t, LANES), jnp.int32),
                   jax.ShapeDtypeStruct((t, LANES), F32),
                   jax.ShapeDtypeStruct((8, LANES), jnp.int32)],
        scratch_shapes=[pltpu.VMEM((8, LANES), F32)],
        compiler_params=_cparams(("arbitrary",)),
        name="merge",
    )(x2, oa, ob, proj, proj, wa, wb, wo, g1, n2, sc2, sh2, wr2, wr_hi, br)


def _row_copy(src_hbm, dst_ref, src_row, dst_row, sem):
    return pltpu.make_async_copy(src_hbm.at[pl.ds(src_row, 1)], dst_ref.at[pl.ds(dst_row, 1)], sem)


def _expert_kernel(be_ref, na_ref, tok0_ref, tokn_ref, h_hbm, wg_ref, bg_ref, wu_ref, bu_ref,
                   wd_ref, bd_ref, ys_ref, xs_scr, xb_scr, sems, *, bm, rows_per_step):
    i = pl.program_id(0)
    f = pl.program_id(1)
    n_act = na_ref[0]
    slot = i % 2

    def wait_slot(s):
        pltpu.make_async_copy(xs_scr.at[s], xs_scr.at[s], sems.at[s]).wait()

    @pl.when(i < n_act)
    def _():
        @pl.when(f == 0)
        def _():
            @pl.when(i == 0)
            def _():
                def issue(r, carry):
                    _row_copy(h_hbm, xs_scr.at[0], tok0_ref[0, 0, r], r, sems.at[0]).start()
                    return carry

                lax.fori_loop(0, bm, issue, 0)

            wait_slot(slot)
            xb_scr[...] = xs_scr[slot].astype(BF16)
            ys_ref[...] = jnp.broadcast_to(bd_ref[0], ys_ref.shape)

        for r in range(rows_per_step):
            _row_copy(h_hbm, xs_scr.at[1 - slot], tokn_ref[0, 0, r], f * rows_per_step + r,
                      sems.at[1 - slot]).start()

        xb = xb_scr[...]
        gt = jnp.minimum(_dot(xb, wg_ref[0].astype(BF16)) + bg_ref[0], SWIGLU_LIMIT)
        up = jnp.clip(_dot(xb, wu_ref[0].astype(BF16)) + bu_ref[0], -SWIGLU_LIMIT, SWIGLU_LIMIT)
        act = (up + 1.0) * gt * jax.nn.sigmoid(SWIGLU_ALPHA * gt)
        ys_ref[...] += _dot(act.astype(BF16), wd_ref[0].astype(BF16))

        @pl.when(jnp.logical_and(f == pl.num_programs(1) - 1, i == n_act - 1))
        def _():
            wait_slot(1 - slot)

    @pl.when(jnp.logical_and(i >= n_act, f == 0))
    def _():
        ys_ref[...] = jnp.zeros(ys_ref.shape, F32)


def _experts(h2, row_token, block_expert, n_active, wg, bg, wu, bu, wd, bd, bm):
    d = h2.shape[1]
    n_exp, _, ff = wg.shape
    tf = EXPERT_FF_COLS
    assert ff % tf == 0 and bm % (ff // tf) == 0
    nf = ff // tf
    rows_per_step = bm // nf
    p = row_token.shape[0]
    n_blocks = p // bm

    def row(i, na):
        return jnp.minimum(i, na[0] - 1)

    def next_rows(i, f, be, na):
        return (jnp.minimum(i + 1, n_blocks - 1) * nf + f, 0, 0)

    kern = functools.partial(_expert_kernel, bm=bm, rows_per_step=rows_per_step)
    grid_spec = pltpu.PrefetchScalarGridSpec(
        num_scalar_prefetch=2,
        grid=(n_blocks, nf),
        in_specs=[pl.BlockSpec((1, 1, bm), lambda i, f, be, na: (0, 0, 0), memory_space=pltpu.SMEM),
                  pl.BlockSpec((1, 1, rows_per_step), next_rows, memory_space=pltpu.SMEM),
                  pl.BlockSpec(memory_space=pl.ANY),
                  pl.BlockSpec((1, d, tf), lambda i, f, be, na: (be[row(i, na)], 0, f)),
                  pl.BlockSpec((1, 1, tf), lambda i, f, be, na: (be[row(i, na)], 0, f)),
                  pl.BlockSpec((1, d, tf), lambda i, f, be, na: (be[row(i, na)], 0, f)),
                  pl.BlockSpec((1, 1, tf), lambda i, f, be, na: (be[row(i, na)], 0, f)),
                  pl.BlockSpec((1, tf, d), lambda i, f, be, na: (be[row(i, na)], f, 0)),
                  pl.BlockSpec((1, 1, d), lambda i, f, be, na: (be[row(i, na)], 0, 0))],
        out_specs=pl.BlockSpec((bm, d), lambda i, f, be, na: (i, 0)),
        scratch_shapes=[pltpu.VMEM((2, bm, d), F32), pltpu.VMEM((bm, d), BF16),
                        pltpu.SemaphoreType.DMA((2,))],
    )
    return pl.pallas_call(
        kern,
        grid_spec=grid_spec,
        out_shape=jax.ShapeDtypeStruct((p, d), F32),
        compiler_params=_cparams(("arbitrary", "arbitrary")),
        name="experts",
    )(block_expert, n_active, row_token.reshape(n_blocks, 1, bm),
      row_token.reshape(n_blocks * nf, 1, rows_per_step), h2,
      wg, bg.reshape(n_exp, 1, ff), wu, bu.reshape(n_exp, 1, ff), wd, bd.reshape(n_exp, 1, d))


def _combine_kernel(pos_ref, x1_ref, tw_ref, g2_ref, ys_hbm, o_ref, rows_scr, sem, *, tc):
    def issue(r, carry):
        for kk in range(TOP_K):
            _row_copy(ys_hbm, rows_scr.at[kk], pos_ref[0, 0, r * TOP_K + kk], r, sem).start(
                priority=kk % 2)
        return carry

    lax.fori_loop(0, tc, issue, 0)
    pltpu.make_async_copy(rows_scr, rows_scr, sem).wait()
    tw = tw_ref[...]
    moe = tw[:, 0:1] * rows_scr[0]
    for kk in range(1, TOP_K):
        moe = moe + tw[:, kk:kk + 1] * rows_scr[kk]
    o_ref[...] = x1_ref[...] + g2_ref[0] * moe


def _combine(x1, top_w, g2, ys, pos, seq):
    t, d = x1.shape
    tc = min(COMBINE_ROWS, seq)
    assert seq % tc == 0
    tpb = seq // tc
    n_steps = t // tc
    kern = functools.partial(_combine_kernel, tc=tc)
    return pl.pallas_call(
        kern,
        grid=(n_steps,),
        in_specs=[pl.BlockSpec((1, 1, tc * TOP_K), lambda i: (i, 0, 0), memory_space=pltpu.SMEM),
                  pl.BlockSpec((tc, d), lambda i: (i, 0)),
                  pl.BlockSpec((tc, LANES), lambda i: (i, 0)),
                  pl.BlockSpec((1, 1, d), lambda i: (i // tpb, 0, 0)),
                  pl.BlockSpec(memory_space=pl.ANY)],
        out_specs=pl.BlockSpec((tc, d), lambda i: (i, 0)),
        out_shape=jax.ShapeDtypeStruct((t, d), F32),
        scratch_shapes=[pltpu.VMEM((TOP_K, tc, d), F32), pltpu.SemaphoreType.DMA(())],
        compiler_params=_cparams(("arbitrary",)),
        name="combine",
    )(pos.reshape(n_steps, 1, tc * TOP_K), x1, top_w, g2, ys)


def _route(top_idx, rank, counts, bm):
    t = top_idx.shape[0]
    n_experts = counts.shape[0]
    m = t * TOP_K
    n_blocks = (m + n_experts * (bm - 1) + bm - 1) // bm
    flat_e = top_idx.reshape(m)
    rank = rank.reshape(m)
    padded = (counts + bm - 1) // bm * bm
    pad_end = jnp.cumsum(padded)
    pad_start = pad_end - padded
    dest = (pad_start[flat_e] + rank).astype(jnp.int32)
    row_token = jnp.zeros((n_blocks * bm,), jnp.int32).at[dest].set(
        jnp.arange(m, dtype=jnp.int32) // TOP_K, unique_indices=True)
    block_start = jnp.arange(n_blocks, dtype=jnp.int32) * bm
    block_expert = jnp.minimum(jnp.sum(pad_end[None, :] <= block_start[:, None], axis=1),
                               n_experts - 1).astype(jnp.int32)
    n_active = (pad_end[-1] // bm).astype(jnp.int32).reshape(1)
    return row_token, dest, block_expert, n_active


def _moe(x1, h2, top_idx, top_w, counts, g2, wg, bg, wu, bu, wd, bd, seq):
    n_experts = wg.shape[0]
    bm = EXPERT_ROWS
    row_token, dest, block_expert, n_active = _route(
        top_idx[:, :TOP_K], top_idx[:, TOP_K:2 * TOP_K], counts[0, :n_experts], bm)
    ys = _experts(h2, row_token, block_expert, n_active, wg, bg, wu, bu, wd, bd, bm)
    return _combine(x1, top_w, g2, ys, dest, seq)


def _layer(x, c, l, w_ada, b_ada, norm1_gain, norm2_gain, w_in, shift_mu, w0, w_decay_up, a0,
           w_aaa_up, w_gate_up, k_k, k_a, r_k, gn_w, gn_b, q_gain, k_gain, lam_q1, lam_k1,
           lam_q2, lam_k2, subln_gain, w_branch_a, w_branch_b, w_out, w_router, b_router,
           w_exp_gate, b_exp_gate, w_exp_up, b_exp_up, w_exp_down, b_exp_down):
    bsz, seq, d = x.shape
    t = bsz * seq
    mod = _ada(c, w_ada, b_ada)
    sh1, sc1, g1, sh2, sc2, g2 = (m.reshape(bsz, 1, d) for m in jnp.split(mod, 6, axis=-1))

    decay_rank, aaa_rank, gate_rank = w_decay_up.shape[0], w_aaa_up.shape[0], w_gate_up.shape[0]
    w_packed, mu_packed = _pack_in_weights(w_in, shift_mu, decay_rank, aaa_rank, gate_rank)
    x2 = x.reshape(t, d)
    proj = _inproj(x2, norm1_gain.reshape(1, d), sc1, sh1, w_packed, mu_packed, seq)
    proj3 = proj.reshape(bsz, seq, PACKED_WIDTH)

    padr = lambda w: jnp.pad(w, ((0, LANES - w.shape[0]), (0, 0))).astype(BF16)
    prm = jnp.stack([w0, a0, k_k, k_a, r_k, gn_w, gn_b, jnp.zeros_like(w0)], axis=0)
    o_a = _rwkv(proj3, padr(w_decay_up), padr(w_aaa_up), w_gate_up.astype(BF16), prm, seq)

    lambda_init = 0.8 - 0.6 * math.exp(-0.3 * l)
    q2, kh, vh = _attn_prep(proj3, q_gain, k_gain, seq)
    lam_rows = jnp.pad(jnp.stack([lam_q1, lam_k1, lam_q2, lam_k2], axis=0),
                       ((0, 4), (0, LANES - DIFF_QK_DIM)))
    o_b = _flash(q2, kh, vh, lam_rows, subln_gain, seq, lambda_init)

    n_experts = w_router.shape[1]
    wr = jnp.pad(w_router, ((0, 0), (0, LANES - n_experts)))
    br = jnp.pad(b_router, (0, LANES - n_experts)).reshape(1, LANES)
    x1, h2, top_idx, top_w, counts = _merge(
        x2, o_a.reshape(t, RWKV_WIDTH), o_b.reshape(t, DIFF_WIDTH), proj,
        w_branch_a.astype(BF16), w_branch_b.astype(BF16), w_out.astype(BF16),
        g1, norm2_gain.reshape(1, d), sc2, sh2, wr, br, seq, n_experts)

    out = _moe(x1, h2, top_idx, top_w, counts, g2, w_exp_gate, b_exp_gate, w_exp_up, b_exp_up,
               w_exp_down, b_exp_down, seq)
    return out.reshape(bsz, seq, d)


def kernel(x, c, w_ada, b_ada, norm1_gain, norm2_gain, w_in, shift_mu, w0, w_decay_up, a0, w_aaa_up, w_gate_up, k_k, k_a, r_k, gn_w, gn_b, q_gain, k_gain, lam_q1, lam_k1, lam_q2, lam_k2, subln_gain, w_branch_a, w_branch_b, w_out, w_router, b_router, w_exp_gate, b_exp_gate, w_exp_up, b_exp_up, w_exp_down, b_exp_down):
    per_layer = (w_ada, b_ada, norm1_gain, norm2_gain, w_in, shift_mu, w0, w_decay_up, a0,
                 w_aaa_up, w_gate_up, k_k, k_a, r_k, gn_w, gn_b, q_gain, k_gain, lam_q1, lam_k1,
                 lam_q2, lam_k2, subln_gain, w_branch_a, w_branch_b, w_out, w_router, b_router,
                 w_exp_gate, b_exp_gate, w_exp_up, b_exp_up, w_exp_down, b_exp_down)
    for l in range(w_in.shape[0]):
        x = _layer(x, c, l, *(p[l] for p in per_layer))
    return x
```

```python
import functools
import math

import jax
import jax.numpy as jnp
from jax import lax
from jax.experimental import pallas as pl
from jax.experimental.pallas import tpu as pltpu

F32 = jnp.float32
BF16 = jnp.bfloat16

NORM_EPS = 1e-6
CHUNK = 64

RWKV_HEADS = 16
RWKV_HEAD_DIM = 64
RWKV_WIDTH = RWKV_HEADS * RWKV_HEAD_DIM
RWKV_GN_EPS = 64e-5

DIFF_HEADS = 8
DIFF_QK_DIM = 64
DIFF_V_DIM = 128
DIFF_WIDTH = DIFF_HEADS * DIFF_V_DIM
ROPE_THETA = 500000.0
ROPE_DIM = DIFF_QK_DIM // 4

TOP_K = 4
SWIGLU_LIMIT = 7.0
SWIGLU_ALPHA = 1.702

LANES = 128
LORA_PACK = 512
VMEM_LIMIT = 56 * 1024 * 1024

ADA_COLS = 1024
INPROJ_ROWS, INPROJ_COLS = 1024, 1536
RWKV_ROWS = 512
PREP_ROWS = 512
FLASH_BLOCK = 1024
MERGE_ROWS = 256
EXPERT_ROWS, EXPERT_FF_COLS = 768, 512
COMBINE_ROWS = 256

OFF_R, OFF_K, OFF_V = 0, 1024, 2048
OFF_QD, OFF_KD, OFF_VD = 3072, 4096, 5120
OFF_GA, OFF_GB = 6144, 8192
OFF_LORA = 10240
PACKED_WIDTH = OFF_LORA + LORA_PACK


def _cparams(sem):
    return pltpu.CompilerParams(dimension_semantics=sem, vmem_limit_bytes=VMEM_LIMIT)


def _nt_dot(a, b):
    return lax.dot_general(a, b, (((1,), (1,)), ((), ())), preferred_element_type=F32)


def _dot(a, b):
    return jnp.dot(a, b, preferred_element_type=F32)


def _ada_kernel(c_ref, w_ref, b_ref, o_ref):
    c = c_ref[...]
    s = c * jax.nn.sigmoid(c)
    o_ref[...] = _dot(s.astype(BF16), w_ref[...].astype(BF16)) + b_ref[...]


def _ada(c, w, b):
    bsz, d = c.shape
    n = w.shape[1]
    rows = 8
    tn = ADA_COLS
    cp = jnp.pad(c, ((0, rows - bsz), (0, 0)))
    out = pl.pallas_call(
        _ada_kernel,
        grid=(n // tn,),
        in_specs=[pl.BlockSpec((rows, d), lambda j: (0, 0)),
                  pl.BlockSpec((d, tn), lambda j: (0, j)),
                  pl.BlockSpec((1, tn), lambda j: (0, j))],
        out_specs=pl.BlockSpec((rows, tn), lambda j: (0, j)),
        out_shape=jax.ShapeDtypeStruct((rows, n), F32),
        compiler_params=_cparams(("arbitrary",)),
        name="ada",
    )(cp, w, b.reshape(1, n))
    return out[:bsz]


def _inproj_kernel(x_ref, g_ref, sc_ref, sh_ref, w_ref, mu_ref, o_ref, h_scr, carry_scr,
                   *, tiles_per_batch, n_rkv_tiles, lora_tile):
    i = pl.program_id(0)
    j = pl.program_id(1)

    @pl.when(j == 0)
    def _():
        x = x_ref[...]
        ms = jnp.mean(x * x, axis=-1, keepdims=True)
        y = x * lax.rsqrt(ms + NORM_EPS) * g_ref[...]
        h_scr[...] = (y * (1.0 + sc_ref[0]) + sh_ref[0]).astype(BF16)

    p = _dot(h_scr[...], w_ref[...])
    shifted = jnp.logical_or(j < n_rkv_tiles, j == lora_tile)

    @pl.when(shifted)
    def _():
        slot = jnp.where(j == lora_tile, n_rkv_tiles, j)

        @pl.when(i % tiles_per_batch == 0)
        def _():
            carry_scr[slot] = jnp.zeros(carry_scr.shape[1:], F32)

        prev_last = carry_scr[slot]
        tm = p.shape[0]
        carry_scr[slot] = p[tm - 1:tm, :]
        row = lax.broadcasted_iota(jnp.int32, p.shape, 0)
        prev = jnp.where(row == 0, prev_last, pltpu.roll(p, 1, 0))
        o_ref[...] = (p + (prev - p) * mu_ref[...]).astype(o_ref.dtype)

    @pl.when(jnp.logical_not(shifted))
    def _():
        o_ref[...] = p.astype(o_ref.dtype)


def _inproj(x2, gain, sc, sh, w_packed, mu_packed, seq):
    t, d = x2.shape
    tm = min(INPROJ_ROWS, seq)
    tn = INPROJ_COLS
    assert seq % tm == 0 and t % tm == 0
    assert PACKED_WIDTH % tn == 0 and OFF_QD % tn == 0 and PACKED_WIDTH - tn <= OFF_LORA
    n_col = PACKED_WIDTH // tn
    n_rkv_tiles = OFF_QD // tn
    lora_tile = n_col - 1
    tiles_per_batch = seq // tm

    kern = functools.partial(_inproj_kernel, tiles_per_batch=tiles_per_batch,
                             n_rkv_tiles=n_rkv_tiles, lora_tile=lora_tile)
    return pl.pallas_call(
        kern,
        grid=(t // tm, n_col),
        in_specs=[pl.BlockSpec((tm, d), lambda i, j: (i, 0)),
                  pl.BlockSpec((1, d), lambda i, j: (0, 0)),
                  pl.BlockSpec((1, 1, d), lambda i, j: (i // tiles_per_batch, 0, 0)),
                  pl.BlockSpec((1, 1, d), lambda i, j: (i // tiles_per_batch, 0, 0)),
                  pl.BlockSpec((d, tn), lambda i, j: (0, j)),
                  pl.BlockSpec((1, tn), lambda i, j: (0, j))],
        out_specs=pl.BlockSpec((tm, tn), lambda i, j: (i, j)),
        out_shape=jax.ShapeDtypeStruct((t, PACKED_WIDTH), BF16),
        scratch_shapes=[pltpu.VMEM((tm, d), BF16),
                        pltpu.VMEM((n_rkv_tiles + 1, 1, tn), F32)],
        compiler_params=_cparams(("arbitrary", "arbitrary")),
        name="inproj",
    )(x2, gain, sc, sh, w_packed, mu_packed)


def _pack_in_weights(w_in, shift_mu, decay_rank, aaa_rank, gate_rank):
    assert decay_rank <= LANES and aaa_rank <= LANES and gate_rank == 2 * LANES
    rw = RWKV_WIDTH
    o = 3 * rw
    w_r, w_k, w_v = w_in[:, 0:rw], w_in[:, rw:2 * rw], w_in[:, 2 * rw:3 * rw]
    w_dl = w_in[:, o:o + decay_rank]
    w_al = w_in[:, o + decay_rank:o + decay_rank + aaa_rank]
    w_gl = w_in[:, o + decay_rank + aaa_rank:o + decay_rank + aaa_rank + gate_rank]
    rest = w_in[:, o + decay_rank + aaa_rank + gate_rank:]

    def padc(a, width):
        return jnp.pad(a, ((0, 0), (0, width - a.shape[1])))

    packed = jnp.concatenate(
        [w_r, w_k, w_v, rest, padc(w_dl, LANES), padc(w_al, LANES), w_gl], axis=1).astype(BF16)
    assert packed.shape[1] == PACKED_WIDTH
    mu = shift_mu.reshape(1, -1)
    mu_packed = jnp.concatenate(
        [mu[:, :o], jnp.zeros((1, OFF_LORA - o), F32), padc(mu[:, o:o + decay_rank], LANES),
         padc(mu[:, o + decay_rank:o + decay_rank + aaa_rank], LANES),
         mu[:, o + decay_rank + aaa_rank:]], axis=1)
    return packed, mu_packed


def _split3_bf16(x):
    hi = x.astype(BF16)
    r1 = x - hi.astype(F32)
    mid = r1.astype(BF16)
    lo = (r1 - mid.astype(F32)).astype(BF16)
    return hi, mid, lo


def _rwkv_kernel(r_ref, k_ref, v_ref, lora_ref, wd_ref, wa_ref, wg_ref, prm_ref, o_ref, st_ref,
                 *, tm, nb):
    c = CHUNK
    n = RWKV_HEAD_DIM
    n_chunks = tm // c

    @pl.when(pl.program_id(1) == 0)
    def _():
        st_ref[...] = jnp.zeros(st_ref.shape, F32)

    lane = lax.broadcasted_iota(jnp.int32, (1, LANES), 1)
    m0 = lane < n
    m0_256 = jnp.concatenate([m0, m0], axis=1)

    def head_sum(x):
        s0 = jnp.sum(jnp.where(m0, x, 0.0), axis=-1, keepdims=True)
        s1 = jnp.sum(jnp.where(m0, 0.0, x), axis=-1, keepdims=True)
        return jnp.where(m0, s0, s1)

    prm = prm_ref[...]
    w0, a0, k_k, k_a, r_k, gn_w, gn_b = (prm[i:i + 1] for i in range(7))

    tb = min(tm, 2 * LANES)
    ri = lax.broadcasted_iota(jnp.int32, (tb, tb), 0)
    ci = lax.broadcasted_iota(jnp.int32, (tb, tb), 1)
    tri = jnp.where(jnp.logical_and(ci <= ri, ci // c == ri // c), 1.0, 0.0).astype(BF16)

    def prep(bb):
        r = r_ref[bb].astype(F32)
        k = k_ref[bb].astype(F32)
        v = v_ref[bb].astype(F32)
        lora = lora_ref[bb]
        d_code = lora[:, 0:LANES].astype(F32)
        a_code = lora[:, LANES:2 * LANES]
        g_code = lora[:, 2 * LANES:4 * LANES].astype(F32)

        wl = w0 + _dot(jnp.tanh(d_code).astype(BF16), wd_ref[...])
        z = -wl
        softplus = jnp.maximum(z, 0.0) + jnp.log(1.0 + jnp.exp(-jnp.abs(z)))
        lw = -jnp.exp(-softplus - 0.5)
        a = jax.nn.sigmoid(a0 + _dot(a_code, wa_ref[...]))
        g = _dot(jax.nn.sigmoid(g_code).astype(BF16), wg_ref[...])

        kk = k * k_k
        kk = kk / jnp.maximum(jnp.sqrt(head_sum(kk * kk)), 1e-12)
        k2 = k * (1.0 + (a - 1.0) * k_a)
        a_vec = -kk
        b_vec = kk * a
        bonus = head_sum(r * k2 * r_k) * v

        lw3 = jnp.concatenate(_split3_bf16(lw), axis=1)
        cum3 = jnp.concatenate([_dot(tri, lw3[u * tb:(u + 1) * tb]) for u in range(tm // tb)], axis=0)
        cum = cum3[:, 0:LANES] + cum3[:, LANES:2 * LANES] + cum3[:, 2 * LANES:3 * LANES]
        cum_end = jnp.concatenate(
            [jnp.broadcast_to(cum[(q + 1) * c - 1:(q + 1) * c, :], (c, LANES)) for q in range(n_chunks)],
            axis=0)
        e_neg = jnp.exp(-cum)
        e_end = jnp.exp(cum_end - cum)
        return dict(a_t=a_vec * jnp.exp(cum - lw), r_t=r * jnp.exp(cum), k_h=k2 * e_neg, b_h=b_vec * e_neg,
                    k_d=k2 * e_end, b_d=b_vec * e_end, v=v, d_end=jnp.exp(cum_end), bonus=bonus, g=g)

    preps = [prep(bb) for bb in range(nb)]

    row64 = lax.broadcasted_iota(jnp.int32, (c, LANES), 0)
    col64 = lax.broadcasted_iota(jnp.int32, (c, LANES), 1) % n
    strict = row64 > col64
    incl = row64 >= col64
    rb = lax.broadcasted_iota(jnp.int32, (LANES, LANES), 0) < n
    cb = lax.broadcasted_iota(jnp.int32, (LANES, LANES), 1) < n
    blockdiag = rb == cb

    def stack_heads(zz, mask):
        return jnp.concatenate([jnp.where(mask, zz, 0.0), jnp.where(mask, 0.0, zz)],
                               axis=0).astype(BF16)

    items = [(q, bb) for q in range(n_chunks) for bb in range(nb)]
    qs = range(len(items))
    ch = lambda name: [preps[bb][name][q * c:(q + 1) * c] for q, bb in items]
    at_c, rt_c, kh_c, bh_c, kd_c, bd_c, v_c = (
        ch(name) for name in ("a_t", "r_t", "k_h", "b_h", "k_d", "b_d", "v"))
    out0 = [_nt_dot(jnp.concatenate([jnp.where(m0, at_c[q], 0.0), jnp.where(m0, rt_c[q], 0.0)],
                                    axis=0).astype(BF16),
                    jnp.concatenate([bh_c[q], kh_c[q]], axis=0).astype(BF16)) for q in qs]
    out1 = [_nt_dot(jnp.concatenate([jnp.where(m0, 0.0, at_c[q]), jnp.where(m0, 0.0, rt_c[q])],
                                    axis=0).astype(BF16),
                    jnp.concatenate([kh_c[q], bh_c[q]], axis=0).astype(BF16)) for q in qs]
    x_cat = [jnp.where(strict, jnp.where(m0, out0[q][:c], out1[q][:c]), 0.0) for q in qs]
    ak_x = [jnp.where(strict, jnp.where(m0, out1[q][:c], out0[q][:c]), 0.0) for q in qs]
    rb_cat = [jnp.where(incl, jnp.where(m0, out0[q][c:], out1[q][c:]), 0.0) for q in qs]
    rk_x = [jnp.where(incl, jnp.where(m0, out1[q][c:], out0[q][c:]), 0.0) for q in qs]
    w12 = [_dot(jnp.concatenate([ak_x[q], rk_x[q]], axis=0).astype(BF16),
                jnp.concatenate([jnp.where(m0, 0.0, v_c[q]), jnp.where(m0, v_c[q], 0.0)],
                                axis=0).astype(BF16)) for q in qs]

    zz = [jnp.concatenate([at_c[q], w12[q][:c]], axis=1) for q in qs]
    xp = x_cat
    for step in range(6):
        zz = [zz[q] + _dot(xp[q].astype(BF16), stack_heads(zz[q], m0_256)) for q in qs]
        if step < 5:
            xp = [_dot(xp[q].astype(BF16), stack_heads(xp[q], m0)) for q in qs]
    rbz = [_dot(rb_cat[q].astype(BF16), stack_heads(zz[q], m0_256)) for q in qs]
    q_c = [rt_c[q] + rbz[q][:, 0:LANES] for q in qs]
    z_c = [w12[q][c:] + rbz[q][:, LANES:] for q in qs]
    mn = []
    for q in qs:
        t_src = jnp.concatenate(
            [zz[q], jnp.concatenate([jnp.zeros_like(v_c[q]), v_c[q]], axis=1)], axis=0)
        mn.append(_dot(t_src.T.astype(BF16),
                       jnp.concatenate([bd_c[q], kd_c[q]], axis=0).astype(BF16)))
    m_x = [jnp.where(blockdiag, mn[q][0:LANES], 0.0).astype(BF16) for q in qs]
    n_x = [jnp.where(blockdiag, mn[q][LANES:], 0.0) for q in qs]

    states = [st_ref[bb] for bb in range(nb)]
    ys = [[] for _ in range(nb)]
    for j, (q, bb) in enumerate(items):
        s_bf = states[bb].astype(BF16)
        ys[bb].append(_nt_dot(q_c[j].astype(BF16), s_bf) + z_c[j])
        states[bb] = (states[bb] * preps[bb]["d_end"][q * c:q * c + 1, :]
                      + _dot(s_bf, m_x[j]) + n_x[j])

    for bb in range(nb):
        st_ref[bb] = states[bb]
        y = jnp.concatenate(ys[bb], axis=0)
        mu = head_sum(y) * (1.0 / n)
        yc = y - mu
        var = head_sum(yc * yc) * (1.0 / n)
        yn = yc * lax.rsqrt(var + RWKV_GN_EPS) * gn_w + gn_b
        o_ref[bb] = ((yn + preps[bb]["bonus"]) * preps[bb]["g"]).astype(o_ref.dtype)


def _rwkv(proj3, wd, wa, wg, prm, seq):
    bsz = proj3.shape[0]
    tm = min(RWKV_ROWS, seq)
    assert seq % tm == 0 and tm % CHUNK == 0
    n_pairs = RWKV_WIDTH // LANES
    blk = lambda off: (lambda p, t: (0, t, off // LANES + p))
    kern = functools.partial(_rwkv_kernel, tm=tm, nb=bsz)
    return pl.pallas_call(
        kern,
        grid=(n_pairs, seq // tm),
        in_specs=[pl.BlockSpec((bsz, tm, LANES), blk(OFF_R)),
                  pl.BlockSpec((bsz, tm, LANES), blk(OFF_K)),
                  pl.BlockSpec((bsz, tm, LANES), blk(OFF_V)),
                  pl.BlockSpec((bsz, tm, LORA_PACK), lambda p, t: (0, t, OFF_LORA // LORA_PACK)),
                  pl.BlockSpec((LANES, LANES), lambda p, t: (0, p)),
                  pl.BlockSpec((LANES, LANES), lambda p, t: (0, p)),
                  pl.BlockSpec((2 * LANES, LANES), lambda p, t: (0, p)),
                  pl.BlockSpec((8, LANES), lambda p, t: (0, p))],
        out_specs=pl.BlockSpec((bsz, tm, LANES), lambda p, t: (0, t, p)),
        out_shape=jax.ShapeDtypeStruct((bsz, seq, RWKV_WIDTH), BF16),
        scratch_shapes=[pltpu.VMEM((bsz, LANES, LANES), F32)],
        compiler_params=_cparams(("arbitrary", "arbitrary")),
        name="rwkv",
    )(proj3, proj3, proj3, proj3, wd, wa, wg, prm)


def _attn_prep_kernel(q_ref, k_ref, v_ref, cos_ref, sa_ref, sb_ref, gavg_ref, qg_ref, kg_ref,
                      q2_ref, ko_ref, vo_ref):
    width = q_ref.shape[-1]
    reps = width // LANES
    tile = lambda t: jnp.concatenate([t] * reps, axis=1)
    cos_t, sin_a, sin_b = tile(cos_ref[...]), tile(sa_ref[...]), tile(sb_ref[...])
    gavg = gavg_ref[...]
    half = ROPE_DIM // 2

    def norm_rope(x, gain):
        x = x.astype(F32)
        sq = x * x
        hi = sq.astype(BF16)
        lo = (sq - hi.astype(F32)).astype(BF16)
        gw = gavg.shape[0]
        ms = jnp.concatenate(
            [_dot(hi[:, u * gw:(u + 1) * gw], gavg) + _dot(lo[:, u * gw:(u + 1) * gw], gavg)
             for u in range(width // gw)], axis=1)
        y = x * lax.rsqrt(ms + NORM_EPS) * gain
        return (y * cos_t + pltpu.roll(y, width - half, 1) * sin_a + pltpu.roll(y, half, 1) * sin_b)

    qn = norm_rope(q_ref[0], qg_ref[...])
    kn = norm_rope(k_ref[0], kg_ref[...])
    lane = lax.broadcasted_iota(jnp.int32, (1, LANES), 1)
    m0 = lane < DIFF_QK_DIM
    for h in range(DIFF_HEADS):
        sl = slice(h * LANES, (h + 1) * LANES)
        qh = qn[:, sl]
        q2_ref[0, h, 0] = jnp.where(m0, qh, 0.0).astype(BF16)
        q2_ref[0, h, 1] = jnp.where(m0, 0.0, qh).astype(BF16)
        ko_ref[0, h] = kn[:, sl].astype(BF16)
        vo_ref[0, h] = v_ref[0, :, sl]


def _rope_tables(seq):
    half = ROPE_DIM // 2
    inv_freq = ROPE_THETA ** (-jnp.arange(half, dtype=F32) * 2.0 / ROPE_DIM)
    ang = jnp.arange(seq, dtype=jnp.int32).astype(F32)[:, None] * inv_freq[None, :]
    cos, sin = jnp.cos(ang), jnp.sin(ang)
    ones = jnp.ones((seq, DIFF_QK_DIM - ROPE_DIM), F32)
    zeros = jnp.zeros((seq, DIFF_QK_DIM - ROPE_DIM), F32)
    zh = jnp.zeros((seq, half), F32)
    cos64 = jnp.concatenate([cos, cos, ones], axis=1)
    sa64 = jnp.concatenate([-sin, zh, zeros], axis=1)
    sb64 = jnp.concatenate([zh, sin, zeros], axis=1)
    dup = lambda t: jnp.concatenate([t, t], axis=1)
    return dup(cos64), dup(sa64), dup(sb64)


def _attn_prep(proj3, q_gain, k_gain, seq):
    bsz = proj3.shape[0]
    tm = min(PREP_ROWS, seq)
    assert seq % tm == 0
    width = DIFF_HEADS * 2 * DIFF_QK_DIM
    cos_t, sin_a, sin_b = _rope_tables(seq)
    gi = jnp.arange(2 * LANES) // DIFF_QK_DIM
    gavg = jnp.where(gi[:, None] == gi[None, :], 1.0 / DIFF_QK_DIM, 0.0).astype(BF16)
    reps = width // DIFF_QK_DIM
    qg = (jnp.tile(q_gain, reps) * (DIFF_QK_DIM ** -0.5 * math.log2(math.e))).reshape(1, width)
    kg = jnp.tile(k_gain, reps).reshape(1, width)
    hd = (bsz, DIFF_HEADS, seq, LANES)
    tab = pl.BlockSpec((tm, LANES), lambda b, t: (t, 0))
    return pl.pallas_call(
        _attn_prep_kernel,
        grid=(bsz, seq // tm),
        in_specs=[pl.BlockSpec((1, tm, width), lambda b, t: (b, t, OFF_QD // width)),
                  pl.BlockSpec((1, tm, width), lambda b, t: (b, t, OFF_KD // width)),
                  pl.BlockSpec((1, tm, width), lambda b, t: (b, t, OFF_VD // width)),
                  tab, tab, tab,
                  pl.BlockSpec((2 * LANES, 2 * LANES), lambda b, t: (0, 0)),
                  pl.BlockSpec((1, width), lambda b, t: (0, 0)),
                  pl.BlockSpec((1, width), lambda b, t: (0, 0))],
        out_specs=[pl.BlockSpec((1, DIFF_HEADS, 2, tm, LANES), lambda b, t: (b, 0, 0, t, 0)),
                   pl.BlockSpec((1, DIFF_HEADS, tm, LANES), lambda b, t: (b, 0, t, 0)),
                   pl.BlockSpec((1, DIFF_HEADS, tm, LANES), lambda b, t: (b, 0, t, 0))],
        out_shape=[jax.ShapeDtypeStruct((bsz, DIFF_HEADS, 2, seq, LANES), BF16),
                   jax.ShapeDtypeStruct(hd, BF16), jax.ShapeDtypeStruct(hd, BF16)],
        compiler_params=_cparams(("arbitrary", "arbitrary")),
        name="attn_prep",
    )(proj3, proj3, proj3, cos_t, sin_a, sin_b, gavg, qg, kg)


def _flash_kernel(qi_ref, kj_ref, q_ref, k_ref, v_ref, lam_ref, sg_ref, o_ref, m_ref, acc_ref,
                  *, tq, tk, lambda_init):
    pidx = pl.program_id(2)
    i = qi_ref[pidx]
    j = kj_ref[pidx]

    @pl.when(j == 0)
    def _():
        m_ref[...] = jnp.full(m_ref.shape, -jnp.inf, F32)
        acc_ref[...] = jnp.zeros(acc_ref.shape, F32)

    q = q_ref[0, 0].reshape(2 * tq, LANES)
    s = _nt_dot(q, k_ref[0, 0]).astype(BF16)

    def update(sc):
        m_old = m_ref[...]
        m_new = jnp.maximum(m_old, jnp.max(sc, axis=-1, keepdims=True).astype(F32))
        alpha = jnp.exp2(m_old - m_new)
        m_b = m_new.astype(BF16)
        p = jnp.exp2(sc - jnp.concatenate([m_b] * (tk // LANES), axis=1))
        v = v_ref[0, 0]
        v_ext = jnp.concatenate([v, jnp.ones_like(v)], axis=1)
        acc_ref[...] = jnp.concatenate([alpha, alpha], axis=1) * acc_ref[...] + _dot(p, v_ext)
        m_ref[...] = m_new

    @pl.when(j < i)
    def _():
        update(s)

    @pl.when(j == i)
    def _():
        qpos = lax.broadcasted_iota(jnp.int32, s.shape, 0) % tq
        kpos = lax.broadcasted_iota(jnp.int32, s.shape, 1)
        update(jnp.where(kpos // CHUNK <= qpos // CHUNK, s, -jnp.inf).astype(BF16))
        lam_rows = lam_ref[...]
        lam = (jnp.exp(jnp.sum(lam_rows[0:1] * lam_rows[1:2], axis=-1, keepdims=True))
               - jnp.exp(jnp.sum(lam_rows[2:3] * lam_rows[3:4], axis=-1, keepdims=True))
               + lambda_init)
        acc = acc_ref[...]
        o1 = acc[0:tq, 0:LANES] / acc[0:tq, LANES:]
        o2 = acc[tq:, 0:LANES] / acc[tq:, LANES:]
        o = o1 - lam * o2
        ms = jnp.mean(o * o, axis=-1, keepdims=True)
        o_ref[0] = (o * lax.rsqrt(ms + NORM_EPS) * sg_ref[...] * (1.0 - lambda_init)).astype(o_ref.dtype)


def _flash(q2, kh, vh, lam_rows, subln_gain, seq, lambda_init):
    bsz = q2.shape[0]
    tq = tk = min(FLASH_BLOCK, seq)
    assert seq % tq == 0 and tq % CHUNK == 0
    nq = seq // tq
    pairs = [(i, j) for i in range(nq) for j in range(i + 1)]
    qi = jnp.asarray([p[0] for p in pairs], jnp.int32)
    kj = jnp.asarray([p[1] for p in pairs], jnp.int32)
    kern = functools.partial(_flash_kernel, tq=tq, tk=tk, lambda_init=lambda_init)
    grid_spec = pltpu.PrefetchScalarGridSpec(
        num_scalar_prefetch=2,
        grid=(bsz, DIFF_HEADS, len(pairs)),
        in_specs=[pl.BlockSpec((1, 1, 2, tq, LANES), lambda b, h, p, qi, kj: (b, h, 0, qi[p], 0)),
                  pl.BlockSpec((1, 1, tk, LANES), lambda b, h, p, qi, kj: (b, h, kj[p], 0)),
                  pl.BlockSpec((1, 1, tk, LANES), lambda b, h, p, qi, kj: (b, h, kj[p], 0)),
                  pl.BlockSpec((8, LANES), lambda b, h, p, qi, kj: (0, 0)),
                  pl.BlockSpec((1, LANES), lambda b, h, p, qi, kj: (0, 0))],
        out_specs=pl.BlockSpec((1, tq, LANES), lambda b, h, p, qi, kj: (b, qi[p], h)),
        scratch_shapes=[pltpu.VMEM((2 * tq, LANES), F32), pltpu.VMEM((2 * tq, 2 * LANES), F32)],
    )
    return pl.pallas_call(
        kern,
        grid_spec=grid_spec,
        out_shape=jax.ShapeDtypeStruct((bsz, seq, DIFF_WIDTH), BF16),
        compiler_params=_cparams(("arbitrary", "arbitrary", "arbitrary")),
        name="flash",
    )(qi, kj, q2, kh, vh, lam_rows, subln_gain.reshape(1, LANES))


def _merge_kernel(x_ref, oa_ref, ob_ref, ga_ref, gb_ref, wa_ref, wb_ref, wo_ref, g1_ref,
                  n2_ref, sc2_ref, sh2_ref, wr2_ref, wrh_ref, br_ref,
                  x1_ref, h2_ref, idx_ref, tw_ref, cnt_ref, cnt_scr, *, n_experts):
    ma = _dot(oa_ref[...], wa_ref[...])
    mb = _dot(ob_ref[...], wb_ref[...])
    merged = (jax.nn.sigmoid(ga_ref[...].astype(F32)) * ma
              + jax.nn.sigmoid(gb_ref[...].astype(F32)) * mb)
    x1 = x_ref[...] + g1_ref[0] * _dot(merged.astype(BF16), wo_ref[...])
    x1_ref[...] = x1
    ms = jnp.mean(x1 * x1, axis=-1, keepdims=True)
    h2 = x1 * lax.rsqrt(ms + NORM_EPS) * n2_ref[...] * (1.0 + sc2_ref[0]) + sh2_ref[0]
    h2_ref[...] = h2

    h_hi = h2.astype(BF16)
    h_lo = (h2 - h_hi.astype(F32)).astype(BF16)
    hh = _dot(h_hi, wr2_ref[...])
    logits = hh[:, 0:LANES] + hh[:, LANES:] + _dot(h_lo, wrh_ref[...]) + br_ref[...]
    lane = lax.broadcasted_iota(jnp.int32, logits.shape, 1)
    lane_f = lane.astype(F32)
    lg = jnp.where(lane < n_experts, logits, -jnp.inf)
    vals, idxs = [], []
    for _ in range(TOP_K):
        mx = jnp.max(lg, axis=-1, keepdims=True)
        ix = jnp.min(jnp.where(lg == mx, lane_f, float(LANES)), axis=-1, keepdims=True)
        vals.append(mx)
        idxs.append(ix)
        lg = jnp.where(lane_f == ix, -jnp.inf, lg)
    exps = [jnp.exp(vv - vals[0]) for vv in vals]
    denom = exps[0] + exps[1] + exps[2] + exps[3]
    idx_out = jnp.zeros(logits.shape, F32)
    tw_out = jnp.zeros(logits.shape, F32)
    for kk in range(TOP_K):
        idx_out = jnp.where(lane == kk, idxs[kk], idx_out)
        tw_out = jnp.where(lane == kk, exps[kk] / denom, tw_out)

    @pl.when(pl.program_id(0) == 0)
    def _():
        cnt_scr[...] = jnp.zeros(cnt_scr.shape, F32)

    tm = logits.shape[0]
    hits = jnp.zeros(logits.shape, F32)
    for kk in range(TOP_K):
        hits = hits + jnp.where(lane_f == idxs[kk], 1.0, 0.0)
    ri = lax.broadcasted_iota(jnp.int32, (tm, tm), 0)
    ci = lax.broadcasted_iota(jnp.int32, (tm, tm), 1)
    before = jnp.where(ci < ri, 1.0, 0.0).astype(BF16)
    base = cnt_scr[0:1, :] + _dot(before, hits.astype(BF16))
    for kk in range(TOP_K):
        rank = jnp.sum(jnp.where(lane_f == idxs[kk], base, 0.0), axis=-1, keepdims=True)
        idx_out = jnp.where(lane == TOP_K + kk, rank, idx_out)
    cnt_new = cnt_scr[...] + jnp.sum(hits, axis=0, keepdims=True)
    cnt_scr[...] = cnt_new
    cnt_ref[...] = cnt_new.astype(jnp.int32)
    idx_ref[...] = idx_out.astype(jnp.int32)
    tw_ref[...] = tw_out


def _merge(x2, oa, ob, proj, wa, wb, wo, g1, n2, sc2, sh2, wr, br, seq, n_experts):
    t, d = x2.shape
    tm = min(MERGE_ROWS, seq)
    assert seq % tm == 0
    tpb = seq // tm
    gw = d
    wr_hi = wr.astype(BF16)
    wr_lo = (wr - wr_hi.astype(F32)).astype(BF16)
    wr2 = jnp.concatenate([wr_hi, wr_lo], axis=1)
    const = lambda shape: pl.BlockSpec(shape, lambda i: tuple(0 for _ in shape),
                                       pipeline_mode=pl.Buffered(1))
    bvec = pl.BlockSpec((1, 1, d), lambda i: (i // tpb, 0, 0))
    kern = functools.partial(_merge_kernel, n_experts=n_experts)
    return pl.pallas_call(
        kern,
        grid=(t // tm,),
        in_specs=[pl.BlockSpec((tm, d), lambda i: (i, 0)),
                  pl.BlockSpec((tm, RWKV_WIDTH), lambda i: (i, 0)),
                  pl.BlockSpec((tm, DIFF_WIDTH), lambda i: (i, 0)),
                  pl.BlockSpec((tm, gw), lambda i: (i, OFF_GA // gw)),
                  pl.BlockSpec((tm, gw), lambda i: (i, OFF_GB // gw)),
                  const(wa.shape), const(wb.shape), const(wo.shape),
                  bvec, const((1, d)), bvec, bvec,
                  const(wr2.shape), const(wr_hi.shape), const(br.shape)],
        out_specs=[pl.BlockSpec((tm, d), lambda i: (i, 0)),
                   pl.BlockSpec((tm, d), lambda i: (i, 0)),
                   pl.BlockSpec((tm, LANES), lambda i: (i, 0)),
                   pl.BlockSpec((tm, LANES), lambda i: (i, 0)),
                   pl.BlockSpec((8, LANES), lambda i: (0, 0))],
        out_shape=[jax.ShapeDtypeStruct((t, d), F32), jax.ShapeDtypeStruct((t, d), F32),
                   jax.ShapeDtypeStruct((t, LANES), jnp.int32),
                   jax.ShapeDtypeStruct((t, LANES), F32),
                   jax.ShapeDtypeStruct((8, LANES), jnp.int32)],
        scratch_shapes=[pltpu.VMEM((8, LANES), F32)],
        compiler_params=_cparams(("arbitrary",)),
        name="merge",
    )(x2, oa, ob, proj, proj, wa, wb, wo, g1, n2, sc2, sh2, wr2, wr_hi, br)


def _row_copy(src_hbm, dst_ref, src_row, dst_row, sem):
    return pltpu.make_async_copy(src_hbm.at[pl.ds(src_row, 1)], dst_ref.at[pl.ds(dst_row, 1)], sem)


def _expert_kernel(be_ref, na_ref, tok0_ref, tokn_ref, h_hbm, wg_ref, bg_ref, wu_ref, bu_ref,
                   wd_ref, bd_ref, ys_ref, xs_scr, xb_scr, sems, *, bm, rows_per_step):
    i = pl.program_id(0)
    f = pl.program_id(1)
    n_act = na_ref[0]
    slot = i % 2

    def wait_slot(s):
        pltpu.make_async_copy(xs_scr.at[s], xs_scr.at[s], sems.at[s]).wait()

    @pl.when(i < n_act)
    def _():
        @pl.when(f == 0)
        def _():
            @pl.when(i == 0)
            def _():
                def issue(r, carry):
                    _row_copy(h_hbm, xs_scr.at[0], tok0_ref[0, 0, r], r, sems.at[0]).start()
                    return carry

                lax.fori_loop(0, bm, issue, 0)

            wait_slot(slot)
            xb_scr[...] = xs_scr[slot].astype(BF16)
            ys_ref[...] = jnp.broadcast_to(bd_ref[0], ys_ref.shape)

        for r in range(rows_per_step):
            _row_copy(h_hbm, xs_scr.at[1 - slot], tokn_ref[0, 0, r], f * rows_per_step + r,
                      sems.at[1 - slot]).start()

        xb = xb_scr[...]
        gt = jnp.minimum(_dot(xb, wg_ref[0].astype(BF16)) + bg_ref[0], SWIGLU_LIMIT)
        up = jnp.clip(_dot(xb, wu_ref[0].astype(BF16)) + bu_ref[0], -SWIGLU_LIMIT, SWIGLU_LIMIT)
        act = (up + 1.0) * gt * jax.nn.sigmoid(SWIGLU_ALPHA * gt)
        ys_ref[...] += _dot(act.astype(BF16), wd_ref[0].astype(BF16))

        @pl.when(jnp.logical_and(f == pl.num_programs(1) - 1, i == n_act - 1))
        def _():
            wait_slot(1 - slot)

    @pl.when(jnp.logical_and(i >= n_act, f == 0))
    def _():
        ys_ref[...] = jnp.zeros(ys_ref.shape, F32)


def _experts(h2, row_token, block_expert, n_active, wg, bg, wu, bu, wd, bd, bm):
    d = h2.shape[1]
    n_exp, _, ff = wg.shape
    tf = EXPERT_FF_COLS
    assert ff % tf == 0 and bm % (ff // tf) == 0
    nf = ff // tf
    rows_per_step = bm // nf
    p = row_token.shape[0]
    n_blocks = p // bm

    def row(i, na):
        return jnp.minimum(i, na[0] - 1)

    def next_rows(i, f, be, na):
        return (jnp.minimum(i + 1, n_blocks - 1) * nf + f, 0, 0)

    kern = functools.partial(_expert_kernel, bm=bm, rows_per_step=rows_per_step)
    grid_spec = pltpu.PrefetchScalarGridSpec(
        num_scalar_prefetch=2,
        grid=(n_blocks, nf),
        in_specs=[pl.BlockSpec((1, 1, bm), lambda i, f, be, na: (0, 0, 0), memory_space=pltpu.SMEM),
                  pl.BlockSpec((1, 1, rows_per_step), next_rows, memory_space=pltpu.SMEM),
                  pl.BlockSpec(memory_space=pl.ANY),
                  pl.BlockSpec((1, d, tf), lambda i, f, be, na: (be[row(i, na)], 0, f)),
                  pl.BlockSpec((1, 1, tf), lambda i, f, be, na: (be[row(i, na)], 0, f)),
                  pl.BlockSpec((1, d, tf), lambda i, f, be, na: (be[row(i, na)], 0, f)),
                  pl.BlockSpec((1, 1, tf), lambda i, f, be, na: (be[row(i, na)], 0, f)),
                  pl.BlockSpec((1, tf, d), lambda i, f, be, na: (be[row(i, na)], f, 0)),
                  pl.BlockSpec((1, 1, d), lambda i, f, be, na: (be[row(i, na)], 0, 0))],
        out_specs=pl.BlockSpec((bm, d), lambda i, f, be, na: (i, 0)),
        scratch_shapes=[pltpu.VMEM((2, bm, d), F32), pltpu.VMEM((bm, d), BF16),
                        pltpu.SemaphoreType.DMA((2,))],
    )
    return pl.pallas_call(
        kern,
        grid_spec=grid_spec,
        out_shape=jax.ShapeDtypeStruct((p, d), F32),
        compiler_params=_cparams(("arbitrary", "arbitrary")),
        name="experts",
    )(block_expert, n_active, row_token.reshape(n_blocks, 1, bm),
      row_token.reshape(n_blocks * nf, 1, rows_per_step), h2,
      wg, bg.reshape(n_exp, 1, ff), wu, bu.reshape(n_exp, 1, ff), wd, bd.reshape(n_exp, 1, d))


def _combine_kernel(pos_ref, x1_ref, tw_ref, g2_ref, ys_hbm, o_ref, rows_scr, sem, *, tc):
    def issue(r, carry):
        for kk in range(TOP_K):
            _row_copy(ys_hbm, rows_scr.at[kk], pos_ref[0, 0, r * TOP_K + kk], r, sem).start(
                priority=kk % 2)
        return carry

    lax.fori_loop(0, tc, issue, 0)
    pltpu.make_async_copy(rows_scr, rows_scr, sem).wait()
    tw = tw_ref[...]
    moe = tw[:, 0:1] * rows_scr[0]
    for kk in range(1, TOP_K):
        moe = moe + tw[:, kk:kk + 1] * rows_scr[kk]
    o_ref[...] = x1_ref[...] + g2_ref[0] * moe


def _combine(x1, top_w, g2, ys, pos, seq):
    t, d = x1.shape
    tc = min(COMBINE_ROWS, seq)
    assert seq % tc == 0
    tpb = seq // tc
    n_steps = t // tc
    kern = functools.partial(_combine_kernel, tc=tc)
    return pl.pallas_call(
        kern,
        grid=(n_steps,),
        in_specs=[pl.BlockSpec((1, 1, tc * TOP_K), lambda i: (i, 0, 0), memory_space=pltpu.SMEM),
                  pl.BlockSpec((tc, d), lambda i: (i, 0)),
                  pl.BlockSpec((tc, LANES), lambda i: (i, 0)),
                  pl.BlockSpec((1, 1, d), lambda i: (i // tpb, 0, 0)),
                  pl.BlockSpec(memory_space=pl.ANY)],
        out_specs=pl.BlockSpec((tc, d), lambda i: (i, 0)),
        out_shape=jax.ShapeDtypeStruct((t, d), F32),
        scratch_shapes=[pltpu.VMEM((TOP_K, tc, d), F32), pltpu.SemaphoreType.DMA(())],
        compiler_params=_cparams(("arbitrary",)),
        name="combine",
    )(pos.reshape(n_steps, 1, tc * TOP_K), x1, top_w, g2, ys)


def _route(top_idx, rank, counts, bm):
    t = top_idx.shape[0]
    n_experts = counts.shape[0]
    m = t * TOP_K
    n_blocks = (m + n_experts * (bm - 1) + bm - 1) // bm
    flat_e = top_idx.reshape(m)
    rank = rank.reshape(m)
    padded = (counts + bm - 1) // bm * bm
    pad_end = jnp.cumsum(padded)
    pad_start = pad_end - padded
    dest = (pad_start[flat_e] + rank).astype(jnp.int32)
    row_token = jnp.zeros((n_blocks * bm,), jnp.int32).at[dest].set(
        jnp.arange(m, dtype=jnp.int32) // TOP_K, unique_indices=True)
    block_start = jnp.arange(n_blocks, dtype=jnp.int32) * bm
    block_expert = jnp.minimum(jnp.sum(pad_end[None, :] <= block_start[:, None], axis=1),
                               n_experts - 1).astype(jnp.int32)
    n_active = (pad_end[-1] // bm).astype(jnp.int32).reshape(1)
    return row_token, dest, block_expert, n_active


def _moe(x1, h2, top_idx, top_w, counts, g2, wg, bg, wu, bu, wd, bd, seq):
    n_experts = wg.shape[0]
    bm = EXPERT_ROWS
    row_token, dest, block_expert, n_active = _route(
        top_idx[:, :TOP_K], top_idx[:, TOP_K:2 * TOP_K], counts[0, :n_experts], bm)
    ys = _experts(h2, row_token, block_expert, n_active, wg, bg, wu, bu, wd, bd, bm)
    return _combine(x1, top_w, g2, ys, dest, seq)


def _layer(x, c, l, w_ada, b_ada, norm1_gain, norm2_gain, w_in, shift_mu, w0, w_decay_up, a0,
           w_aaa_up, w_gate_up, k_k, k_a, r_k, gn_w, gn_b, q_gain, k_gain, lam_q1, lam_k1,
           lam_q2, lam_k2, subln_gain, w_branch_a, w_branch_b, w_out, w_router, b_router,
           w_exp_gate, b_exp_gate, w_exp_up, b_exp_up, w_exp_down, b_exp_down):
    bsz, seq, d = x.shape
    t = bsz * seq
    mod = _ada(c, w_ada, b_ada)
    sh1, sc1, g1, sh2, sc2, g2 = (m.reshape(bsz, 1, d) for m in jnp.split(mod, 6, axis=-1))

    decay_rank, aaa_rank, gate_rank = w_decay_up.shape[0], w_aaa_up.shape[0], w_gate_up.shape[0]
    w_packed, mu_packed = _pack_in_weights(w_in, shift_mu, decay_rank, aaa_rank, gate_rank)
    x2 = x.reshape(t, d)
    proj = _inproj(x2, norm1_gain.reshape(1, d), sc1, sh1, w_packed, mu_packed, seq)
    proj3 = proj.reshape(bsz, seq, PACKED_WIDTH)

    padr = lambda w: jnp.pad(w, ((0, LANES - w.shape[0]), (0, 0))).astype(BF16)
    prm = jnp.stack([w0, a0, k_k, k_a, r_k, gn_w, gn_b, jnp.zeros_like(w0)], axis=0)
    o_a = _rwkv(proj3, padr(w_decay_up), padr(w_aaa_up), w_gate_up.astype(BF16), prm, seq)

    lambda_init = 0.8 - 0.6 * math.exp(-0.3 * l)
    q2, kh, vh = _attn_prep(proj3, q_gain, k_gain, seq)
    lam_rows = jnp.pad(jnp.stack([lam_q1, lam_k1, lam_q2, lam_k2], axis=0),
                       ((0, 4), (0, LANES - DIFF_QK_DIM)))
    o_b = _flash(q2, kh, vh, lam_rows, subln_gain, seq, lambda_init)

    n_experts = w_router.shape[1]
    wr = jnp.pad(w_router, ((0, 0), (0, LANES - n_experts)))
    br = jnp.pad(b_router, (0, LANES - n_experts)).reshape(1, LANES)
    x1, h2, top_idx, top_w, counts = _merge(
        x2, o_a.reshape(t, RWKV_WIDTH), o_b.reshape(t, DIFF_WIDTH), proj,
        w_branch_a.astype(BF16), w_branch_b.astype(BF16), w_out.astype(BF16),
        g1, norm2_gain.reshape(1, d), sc2, sh2, wr, br, seq, n_experts)

    out = _moe(x1, h2, top_idx, top_w, counts, g2, w_exp_gate, b_exp_gate, w_exp_up, b_exp_up,
               w_exp_down, b_exp_down, seq)
    return out.reshape(bsz, seq, d)


def kernel(x, c, w_ada, b_ada, norm1_gain, norm2_gain, w_in, shift_mu, w0, w_decay_up, a0, w_aaa_up, w_gate_up, k_k, k_a, r_k, gn_w, gn_b, q_gain, k_gain, lam_q1, lam_k1, lam_q2, lam_k2, subln_gain, w_branch_a, w_branch_b, w_out, w_router, b_router, w_exp_gate, b_exp_gate, w_exp_up, b_exp_up, w_exp_down, b_exp_down):
    per_layer = (w_ada, b_ada, norm1_gain, norm2_gain, w_in, shift_mu, w0, w_decay_up, a0,
                 w_aaa_up, w_gate_up, k_k, k_a, r_k, gn_w, gn_b, q_gain, k_gain, lam_q1, lam_k1,
                 lam_q2, lam_k2, subln_gain, w_branch_a, w_branch_b, w_out, w_router, b_router,
                 w_exp_gate, b_exp_gate, w_exp_up, b_exp_up, w_exp_down, b_exp_down)
    for l in range(w_in.shape[0]):
        x = _layer(x, c, l, *(p[l] for p in per_layer))
    return x
```
